```python
import numpy as np
import jax
import jax.numpy as jnp
from jax import lax

D_MODEL = 1024
BATCH = 8
SEQ = 4096
DEPTH = 2

ROPE_THETA = 10000.0
NORM_EPS = 1e-6
NEG_INF = -1e30
TINY = 1e-30
D_FF = 2816
Q_BLOCK = 128

MLA_HEADS = 8
MLA_NOPE = 64
MLA_ROPE = 32
MLA_V = 64
MLA_Q_LORA = 256
MLA_KV_LORA = 128

GLA_HEADS = 4
GLA_DK = 64
GLA_DV = 128
GLA_GATE_RANK = 16
GLA_GATE_TAU = 16.0
GLA_CHUNK = 64

NSA_HEADS = 16
NSA_KV_GROUPS = 4
NSA_HEAD_DIM = 64
NSA_CMP_LEN = 32
NSA_CMP_STRIDE = 16
NSA_CMP_HIDDEN = 128
NSA_SEL_LEN = 64
NSA_TOP_N = 16
NSA_WINDOW = 512
NSA_Q_BLOCK = 64
NSA_FORCE_SCORE = 1e4

HY_SPLITS = (MLA_Q_LORA, MLA_KV_LORA, MLA_ROPE, GLA_HEADS * GLA_DK, GLA_HEADS * GLA_DK,
             GLA_HEADS * GLA_DV, GLA_GATE_RANK, GLA_HEADS * GLA_DV)
HY_IN = sum(HY_SPLITS)
HY_OUT = MLA_HEADS * MLA_V + GLA_HEADS * GLA_DV
NSA_KV_WIDTH = NSA_KV_GROUPS * NSA_HEAD_DIM
NSA_SPLITS = (NSA_HEADS * NSA_HEAD_DIM,) + (NSA_KV_WIDTH,) * 6 + (NSA_HEADS * 3,)
NSA_IN = sum(NSA_SPLITS)
NSA_OUT = NSA_HEADS * NSA_HEAD_DIM
N_EVEN = (DEPTH + 1) // 2
N_ODD = DEPTH // 2

kernel_name = 'hybrid_mla_gla_nsa_macaron'


def split_cols(h, sizes):
    return jnp.split(h, np.cumsum(sizes)[:-1].tolist(), axis=-1)


def rms_norm(x, g):
    xf = x.astype(jnp.float32)
    y = xf * lax.rsqrt(jnp.mean(xf * xf, axis=-1, keepdims=True) + NORM_EPS)
    return (y * g.astype(jnp.float32)).astype(x.dtype)


def rope(x, pos):
    d = x.shape[-1]
    inv_freq = jnp.power(ROPE_THETA, -jnp.arange(d // 2, dtype=jnp.float32) * (2.0 / d))
    ang = pos.astype(jnp.float32)[..., None] * inv_freq
    cos = jnp.cos(ang)[:, :, None, :]
    sin = jnp.sin(ang)[:, :, None, :]
    xf = x.astype(jnp.float32)
    x1, x2 = xf[..., : d // 2], xf[..., d // 2:]
    return jnp.concatenate([x1 * cos - x2 * sin, x2 * cos + x1 * sin], axis=-1).astype(x.dtype)


def masked_softmax(s, mask):
    s = jnp.where(mask, s.astype(jnp.float32), NEG_INF)
    m = jnp.max(s, axis=-1, keepdims=True)
    p = jnp.where(mask, jnp.exp(s - m), 0.0)
    return p / jnp.maximum(jnp.sum(p, axis=-1, keepdims=True), TINY)


def swiglu(x, w_gate, w_up, w_down):
    return (jax.nn.silu(x @ w_gate) * (x @ w_up)) @ w_down


def causal_block_attention(q, k, v, scale):
    B, S, H, Dk = q.shape
    nb = S // Q_BLOCK
    q_blocks = q.reshape(B, nb, Q_BLOCK, H, Dk).transpose(1, 0, 2, 3, 4)
    key_pos = jnp.arange(S)

    def one_block(args):
        i, qb = args
        s = jnp.einsum('bqhd,bkhd->bhqk', qb, k, preferred_element_type=jnp.float32) * scale
        q_pos = i * Q_BLOCK + jnp.arange(Q_BLOCK)
        p = masked_softmax(s, key_pos[None, :] <= q_pos[:, None])
        return jnp.einsum('bhqk,bkhd->bqhd', p.astype(v.dtype), v)

    out = lax.map(one_block, (jnp.arange(nb), q_blocks))
    return out.transpose(1, 0, 2, 3, 4).reshape(B, S, H, v.shape[-1])


def mla_heads(c_q, c_kv, k_r, positions, q_norm, w_uq, kv_norm, w_ukv):
    B, S, _ = c_q.shape
    q = (rms_norm(c_q, q_norm) @ w_uq).reshape(B, S, MLA_HEADS, MLA_NOPE + MLA_ROPE)
    q_nope, q_rot = q[..., :MLA_NOPE], rope(q[..., MLA_NOPE:], positions)
    kv = (rms_norm(c_kv, kv_norm) @ w_ukv).reshape(B, S, MLA_HEADS, MLA_NOPE + MLA_V)
    k_nope, v = kv[..., :MLA_NOPE], kv[..., MLA_NOPE:]
    k_rot = rope(k_r.reshape(B, S, 1, MLA_ROPE), positions)
    q_full = jnp.concatenate([q_nope, q_rot], axis=-1)
    k_full = jnp.concatenate([k_nope, jnp.broadcast_to(k_rot, (B, S, MLA_HEADS, MLA_ROPE))], axis=-1)
    o = causal_block_attention(q_full, k_full, v, (MLA_NOPE + MLA_ROPE) ** -0.5)
    return o.reshape(B, S, MLA_HEADS * MLA_V)


def gla_heads(h_q, h_k, h_v, h_a, h_r, w_a2, b_a, out_norm):
    B, S, _ = h_q.shape
    H, DK, DV, C = GLA_HEADS, GLA_DK, GLA_DV, GLA_CHUNK
    nc = S // C
    f32 = jnp.float32
    q = h_q.reshape(B, nc, C, H, DK).astype(f32) * DK ** -0.5
    k = h_k.reshape(B, nc, C, H, DK).astype(f32)
    v = h_v.reshape(B, nc, C, H, DV).astype(f32)
    log_a = jax.nn.log_sigmoid((h_a @ w_a2 + b_a).astype(f32)) / GLA_GATE_TAU
    b = jnp.cumsum(log_a.reshape(B, nc, C, H, DK), axis=2)
    b_last = b[:, :, -1:]
    k_dec = k * jnp.exp(b_last - b)
    q_inter = q * jnp.exp(b)
    q_intra = q * jnp.exp(b - b_last)
    causal = jnp.tril(jnp.ones((C, C), dtype=bool))
    A = jnp.where(causal, jnp.einsum('bnihd,bnjhd->bnhij', q_intra, k_dec), 0.0)
    o_intra = jnp.einsum('bnhij,bnjhv->bnihv', A, v)
    kv_chunk = jnp.einsum('bnjhd,bnjhv->nbhdv', k_dec, v)
    decay_chunk = jnp.exp(b[:, :, -1]).transpose(1, 0, 2, 3)

    def step(state, inp):
        dec, kv = inp
        return state * dec[..., None] + kv, state

    _, s_prev = lax.scan(step, jnp.zeros((B, H, DK, DV), f32), (decay_chunk, kv_chunk))
    o_inter = jnp.einsum('bnihd,nbhdv->bnihv', q_inter, s_prev)
    o = rms_norm((o_intra + o_inter).reshape(B, S, H, DV), out_norm)
    o = o * jax.nn.silu(h_r.reshape(B, S, H, DV).astype(f32))
    return o.reshape(B, S, H * DV).astype(h_q.dtype)


def mla_gla_mixer(u, positions, w_in, q_norm, w_uq, kv_norm, w_ukv, w_a2, b_a, out_norm, w_out):
    c_q, c_kv, k_r, g_q, g_k, g_v, g_a, g_r = split_cols(u @ w_in, HY_SPLITS)
    o_mla = mla_heads(c_q, c_kv, k_r, positions, q_norm, w_uq, kv_norm, w_ukv)
    o_gla = gla_heads(g_q, g_k, g_v, g_a, g_r, w_a2, b_a, out_norm)
    return jnp.concatenate([o_mla, o_gla.astype(o_mla.dtype)], axis=-1) @ w_out


def nsa_mixer(u, positions, w_in, pos_k, pos_v, ck_w1, ck_w2, cv_w1, cv_w2, w_out):
    B, S, _ = u.shape
    H, G, Dh = NSA_HEADS, NSA_KV_GROUPS, NSA_HEAD_DIM
    Hg = H // G
    dt = u.dtype
    q, k_c, v_c, k_s, v_s, k_w, v_w, g = split_cols(u @ w_in, NSA_SPLITS)
    q = rope(q.reshape(B, S, H, Dh), positions)
    gates = jax.nn.sigmoid(g.reshape(B, S, H, 3))

    n_cmp = (S - NSA_CMP_LEN) // NSA_CMP_STRIDE + 1
    cmp_start = np.arange(n_cmp) * NSA_CMP_STRIDE
    cmp_end = cmp_start + NSA_CMP_LEN - 1
    cmp_idx = cmp_start[:, None] + np.arange(NSA_CMP_LEN)[None, :]

    def compress(t, pos_emb, w1, w2):
        blocks = t.reshape(B, S, G, Dh)[:, cmp_idx] + pos_emb[None, None, :, None, :]
        flat = blocks.transpose(0, 1, 3, 2, 4).reshape(B, n_cmp, G, NSA_CMP_LEN * Dh)
        return jax.nn.silu(flat @ w1) @ w2

    k_cmp = rope(compress(k_c, pos_k, ck_w1, ck_w2), positions[:, cmp_end])
    v_cmp = compress(v_c, pos_v, cv_w1, cv_w2)
    cmp_end_j = jnp.asarray(cmp_end)

    n_blk = S // NSA_SEL_LEN
    n_pad = max(n_blk, NSA_TOP_N)

    def to_blocks(t):
        tb = t.reshape(B, n_blk, NSA_SEL_LEN, G, Dh).transpose(0, 3, 1, 2, 4)
        return jnp.pad(tb, ((0, 0), (0, 0), (0, n_pad - n_blk), (0, 0), (0, 0)))

    k_sel_all = to_blocks(rope(k_s.reshape(B, S, G, Dh), positions))
    v_sel_all = to_blocks(v_s.reshape(B, S, G, Dh))
    blk_start = np.arange(n_pad) * NSA_SEL_LEN
    ov = np.minimum(cmp_start[:, None] + NSA_CMP_LEN, blk_start[None, :] + NSA_SEL_LEN) - np.maximum(cmp_start[:, None], blk_start[None, :])
    overlap = jnp.asarray(np.clip(ov, 0, None) / NSA_CMP_LEN, dtype=jnp.float32)

    pad_w = ((0, 0), (NSA_WINDOW, 0), (0, 0), (0, 0))
    k_win = jnp.pad(rope(k_w.reshape(B, S, G, Dh), positions), pad_w)
    v_win = jnp.pad(v_w.reshape(B, S, G, Dh), pad_w)

    nq = S // NSA_Q_BLOCK
    q_blocks = q.reshape(B, nq, NSA_Q_BLOCK, G, Hg, Dh).transpose(1, 0, 2, 3, 4, 5)
    g_blocks = gates.reshape(B, nq, NSA_Q_BLOCK, G, Hg, 3).transpose(1, 0, 2, 3, 4, 5)
    bi = jnp.arange(B)[:, None, None, None]
    gi = jnp.arange(G)[None, :, None, None]
    scale = Dh ** -0.5
    span = NSA_Q_BLOCK + NSA_WINDOW

    def one_block(args):
        i, qb, gb = args
        t = i * NSA_Q_BLOCK + jnp.arange(NSA_Q_BLOCK)
        s_c = jnp.einsum('bqghd,bngd->bghqn', qb, k_cmp, preferred_element_type=jnp.float32) * scale
        p_c = masked_softmax(s_c, cmp_end_j[None, :] <= t[:, None])
        o_c = jnp.einsum('bghqn,bngd->bqghd', p_c.astype(dt), v_cmp)
        imp = jnp.einsum('bghqn,nm->bgqm', p_c, overlap)
        blk = jnp.arange(n_pad)[None, :]
        cur = (t // NSA_SEL_LEN)[:, None]
        forced = (blk == 0) | (blk == cur) | (blk == cur - 1)
        score = jnp.where(blk <= cur, jnp.where(forced, NSA_FORCE_SCORE, imp), NEG_INF)
        _, sel = lax.top_k(score, NSA_TOP_N)
        k_sel = k_sel_all[bi, gi, sel].reshape(B, G, NSA_Q_BLOCK, NSA_TOP_N * NSA_SEL_LEN, Dh)
        v_sel = v_sel_all[bi, gi, sel].reshape(B, G, NSA_Q_BLOCK, NSA_TOP_N * NSA_SEL_LEN, Dh)
        kpos = (sel[..., None] * NSA_SEL_LEN + jnp.arange(NSA_SEL_LEN)).reshape(B, G, NSA_Q_BLOCK, -1)
        s_s = jnp.einsum('bqghd,bgqmd->bghqm', qb, k_sel, preferred_element_type=jnp.float32) * scale
        p_s = masked_softmax(s_s, (kpos <= t[None, None, :, None])[:, :, None])
        o_s = jnp.einsum('bghqm,bgqmd->bqghd', p_s.astype(dt), v_sel)
        start = i * NSA_Q_BLOCK
        k_wb = lax.dynamic_slice_in_dim(k_win, start, span, axis=1)
        v_wb = lax.dynamic_slice_in_dim(v_win, start, span, axis=1)
        wpos = (start - NSA_WINDOW + jnp.arange(span))[None, :]
        m_w = (wpos <= t[:, None]) & (wpos > t[:, None] - NSA_WINDOW) & (wpos >= 0)
        s_w = jnp.einsum('bqghd,bkgd->bghqk', qb, k_wb, preferred_element_type=jnp.float32) * scale
        p_w = masked_softmax(s_w, m_w)
        o_w = jnp.einsum('bghqk,bkgd->bqghd', p_w.astype(dt), v_wb)
        return gb[..., 0:1] * o_c + gb[..., 1:2] * o_s + gb[..., 2:3] * o_w

    out = lax.map(one_block, (jnp.arange(nq), q_blocks, g_blocks))
    return out.transpose(1, 0, 2, 3, 4, 5).reshape(B, S, NSA_OUT) @ w_out


def setup_inputs(seed: int = 0) -> dict:
    key = jax.random.key(seed)
    ks = iter(jax.random.split(key, 40))
    f32 = jnp.float32

    def w(shape, fan_in):
        return jax.random.normal(next(ks), shape, f32) * fan_in ** -0.5

    def gain(shape):
        return 1.0 + 0.02 * jax.random.normal(next(ks), shape, f32)

    def small(shape, s):
        return s * jax.random.normal(next(ks), shape, f32)

    x = jax.random.normal(next(ks), (BATCH, SEQ, D_MODEL), f32)
    offsets = jax.random.randint(next(ks), (BATCH, 1), 0, 1024)
    positions = (offsets + jnp.arange(SEQ)[None, :]).astype(jnp.int32)
    return {
        'x': x,
        'positions': positions,
        'ffn_norm': gain((DEPTH, 2, D_MODEL)),
        'ffn_w_gate': w((DEPTH, 2, D_MODEL, D_FF), D_MODEL),
        'ffn_w_up': w((DEPTH, 2, D_MODEL, D_FF), D_MODEL),
        'ffn_w_down': w((DEPTH, 2, D_FF, D_MODEL), D_FF),
        'mix_norm': gain((DEPTH, D_MODEL)),
        'hy_w_in': w((N_EVEN, D_MODEL, HY_IN), D_MODEL),
        'mla_q_norm': gain((N_EVEN, MLA_Q_LORA)),
        'mla_w_uq': w((N_EVEN, MLA_Q_LORA, MLA_HEADS * (MLA_NOPE + MLA_ROPE)), MLA_Q_LORA),
        'mla_kv_norm': gain((N_EVEN, MLA_KV_LORA)),
        'mla_w_ukv': w((N_EVEN, MLA_KV_LORA, MLA_HEADS * (MLA_NOPE + MLA_V)), MLA_KV_LORA),
        'gla_w_a2': w((N_EVEN, GLA_GATE_RANK, GLA_HEADS * GLA_DK), GLA_GATE_RANK),
        'gla_b_a': small((N_EVEN, GLA_HEADS * GLA_DK), 0.1),
        'gla_out_norm': gain((N_EVEN, GLA_DV)),
        'hy_w_out': w((N_EVEN, HY_OUT, D_MODEL), HY_OUT),
        'nsa_w_in': w((N_ODD, D_MODEL, NSA_IN), D_MODEL),
        'nsa_pos_k': small((N_ODD, NSA_CMP_LEN, NSA_HEAD_DIM), 0.1),
        'nsa_pos_v': small((N_ODD, NSA_CMP_LEN, NSA_HEAD_DIM), 0.1),
        'nsa_ck_w1': w((N_ODD, NSA_CMP_LEN * NSA_HEAD_DIM, NSA_CMP_HIDDEN), NSA_CMP_LEN * NSA_HEAD_DIM),
        'nsa_ck_w2': w((N_ODD, NSA_CMP_HIDDEN, NSA_HEAD_DIM), NSA_CMP_HIDDEN),
        'nsa_cv_w1': w((N_ODD, NSA_CMP_LEN * NSA_HEAD_DIM, NSA_CMP_HIDDEN), NSA_CMP_LEN * NSA_HEAD_DIM),
        'nsa_cv_w2': w((N_ODD, NSA_CMP_HIDDEN, NSA_HEAD_DIM), NSA_CMP_HIDDEN),
        'nsa_w_out': w((N_ODD, NSA_OUT, D_MODEL), NSA_OUT),
        'final_norm': gain((D_MODEL,)),
    }


def reference(x, positions, ffn_norm, ffn_w_gate, ffn_w_up, ffn_w_down, mix_norm,
              hy_w_in, mla_q_norm, mla_w_uq, mla_kv_norm, mla_w_ukv, gla_w_a2, gla_b_a,
              gla_out_norm, hy_w_out, nsa_w_in, nsa_pos_k, nsa_pos_v, nsa_ck_w1, nsa_ck_w2,
              nsa_cv_w1, nsa_cv_w2, nsa_w_out, final_norm):
    h = x
    for layer in range(DEPTH):
        j = layer // 2
        h = h + 0.5 * swiglu(rms_norm(h, ffn_norm[layer, 0]), ffn_w_gate[layer, 0], ffn_w_up[layer, 0], ffn_w_down[layer, 0])
        u = rms_norm(h, mix_norm[layer])
        if layer % 2 == 0:
            mixed = mla_gla_mixer(u, positions, hy_w_in[j], mla_q_norm[j], mla_w_uq[j], mla_kv_norm[j],
                                  mla_w_ukv[j], gla_w_a2[j], gla_b_a[j], gla_out_norm[j], hy_w_out[j])
        else:
            mixed = nsa_mixer(u, positions, nsa_w_in[j], nsa_pos_k[j], nsa_pos_v[j], nsa_ck_w1[j],
                              nsa_ck_w2[j], nsa_cv_w1[j], nsa_cv_w2[j], nsa_w_out[j])
        h = h + mixed.astype(h.dtype)
        h = h + 0.5 * swiglu(rms_norm(h, ffn_norm[layer, 1]), ffn_w_gate[layer, 1], ffn_w_up[layer, 1], ffn_w_down[layer, 1])
    return rms_norm(h, final_norm)
```

```python
import functools

import numpy as np
import jax
import jax.numpy as jnp
from jax import lax
from jax.experimental import pallas as pl
from jax.experimental.pallas import tpu as pltpu

F32 = jnp.float32
CDT = jnp.bfloat16

D_MODEL = 1024
D_FF = 2816
ROPE_THETA = 10000.0
NORM_EPS = 1e-6
NEG_INF = -1e30
TINY = 1e-30

MLA_HEADS, MLA_NOPE, MLA_ROPE, MLA_V = 8, 64, 32, 64
MLA_Q_LORA, MLA_KV_LORA = 256, 128
GLA_HEADS, GLA_DK, GLA_DV = 4, 64, 128
GLA_GATE_RANK, GLA_GATE_TAU, GLA_CHUNK = 16, 16.0, 64
NSA_HEADS, NSA_KV_GROUPS, NSA_HEAD_DIM = 16, 4, 64
NSA_HG = NSA_HEADS // NSA_KV_GROUPS
NSA_CMP_LEN, NSA_CMP_STRIDE, NSA_CMP_HIDDEN = 32, 16, 128
NSA_SEL_LEN, NSA_TOP_N, NSA_WINDOW = 64, 16, 512
NSA_FORCE_SCORE = 1e4

LANES = 128
VMEM_LIMIT = 52 * 1024 * 1024

TM_FFN = 512
TM_PROJ = 512
TQ_MLA = 512
TQ_NSA = 128
TK_NSA = 256
TKW_NSA = 128
FF_CHUNK = 256


def _cparams(sem):
    return pltpu.CompilerParams(dimension_semantics=sem, vmem_limit_bytes=VMEM_LIMIT)


def _full(shape):
    n = len(shape)
    return pl.BlockSpec(shape, lambda *_: (0,) * n)


def _rms(x, g):
    return x * lax.rsqrt(jnp.mean(x * x, axis=-1, keepdims=True) + NORM_EPS) * g


def _dot(a, b):
    return jnp.dot(a, b, preferred_element_type=F32)


def _dot_nt(a, b):
    return lax.dot_general(a, b, (((1,), (1,)), ((), ())), preferred_element_type=F32)


def _dot_tn(a, b):
    return lax.dot_general(a, b, (((0,), (0,)), ((), ())), preferred_element_type=F32)


def _split3(x):
    hi = x.astype(jnp.bfloat16)
    r = x - hi.astype(F32)
    mid = r.astype(jnp.bfloat16)
    lo = (r - mid.astype(F32)).astype(jnp.bfloat16)
    return hi, mid, lo


def _silu(x):
    return x * jax.nn.sigmoid(x)


def _rope128(x, c, s1, s2, sh_a, sh_b):
    outs = []
    for j in range(x.shape[1] // LANES):
        xs = x[:, j * LANES:(j + 1) * LANES]
        outs.append(xs * c + pltpu.roll(xs, sh_a, 1) * s1 + pltpu.roll(xs, sh_b, 1) * s2)
    return outs[0] if len(outs) == 1 else jnp.concatenate(outs, axis=1)


def _rope_tab_body(pos_ref, pat_ref, o_ref):
    pos = pos_ref[...].astype(F32)
    for t in range(2):
        ang = pos * pat_ref[3 * t:3 * t + 1, :]
        c, s = jnp.cos(ang), jnp.sin(ang)
        o_ref[3 * t] = c
        o_ref[3 * t + 1] = -s * pat_ref[3 * t + 1:3 * t + 2, :]
        o_ref[3 * t + 2] = s * pat_ref[3 * t + 2:3 * t + 3, :]


def _rope_tables(positions):
    T = positions.size
    tm = 1024
    lane = np.arange(LANES)
    pat = np.zeros((8, LANES), np.float32)
    j = np.where((lane >= 64) & (lane < 80), lane - 64, np.where((lane >= 80) & (lane < 96), lane - 80, 0))
    in_rope = (lane >= 64) & (lane < 96)
    pat_mla = jnp.where(in_rope, jnp.power(ROPE_THETA, -jnp.asarray(j, F32) * (2.0 / MLA_ROPE)), 0.0)
    pat_nsa = jnp.power(ROPE_THETA, -jnp.asarray(lane % 32, F32) * (2.0 / NSA_HEAD_DIM))
    pat[1] = (lane >= 64) & (lane < 80)
    pat[2] = (lane >= 80) & (lane < 96)
    pat[4] = (lane % 64) < 32
    pat[5] = (lane % 64) >= 32
    pat = jnp.asarray(pat).at[0].set(pat_mla).at[3].set(pat_nsa)
    return pl.pallas_call(
        _rope_tab_body,
        grid=(T // tm,),
        in_specs=[pl.BlockSpec((tm, 1), lambda i: (i, 0)), _full((8, LANES))],
        out_specs=pl.BlockSpec((6, tm, LANES), lambda i: (0, i, 0)),
        out_shape=jax.ShapeDtypeStruct((6, T, LANES), F32),
        compiler_params=_cparams(("parallel",)),
        name="rope_tables",
    )(positions.reshape(T, 1), pat)


def _ffn_body(x_ref, g_ref, wg_ref, wu_ref, wd_ref, fg_ref, o_ref, *, final):
    x = x_ref[...]
    xn = _rms(x, g_ref[...]).astype(CDT)
    acc = jnp.zeros(x.shape, F32)
    for c in range(D_FF // FF_CHUNK):
        sl = slice(c * FF_CHUNK, (c + 1) * FF_CHUNK)
        g = _dot(xn, wg_ref[:, sl])
        u = _dot(xn, wu_ref[:, sl])
        acc = acc + _dot((_silu(g) * u).astype(CDT), wd_ref[sl, :])
    y = x + 0.5 * acc
    if final:
        y = _rms(y, fg_ref[...])
    o_ref[...] = y


def _ffn(h, gain, wg, wu, wd, final_gain=None):
    T, D = h.shape
    tm = TM_FFN
    final = final_gain is not None
    fg = final_gain if final else gain
    wspec = lambda shape: pl.BlockSpec(shape, lambda i: (0, 0), pipeline_mode=pl.Buffered(1))
    return pl.pallas_call(
        functools.partial(_ffn_body, final=final),
        grid=(T // tm,),
        in_specs=[pl.BlockSpec((tm, D), lambda i: (i, 0)), _full((1, D)),
                  wspec((D, D_FF)), wspec((D, D_FF)), wspec((D_FF, D)), _full((1, D))],
        out_specs=pl.BlockSpec((tm, D), lambda i: (i, 0)),
        out_shape=jax.ShapeDtypeStruct((T, D), F32),
        compiler_params=_cparams(("parallel",)),
        name="ffn_final" if final else "ffn",
    )(h, gain.reshape(1, D), wg.astype(CDT), wu.astype(CDT), wd.astype(CDT), fg.reshape(1, D))


HY_NP = 2176


def _hy_prep_body(h_ref, mg_ref, wp_ref, qn_ref, wuq_ref, kvn_ref, wuk_ref, wuv_ref, wa2_ref, ba_ref, tab_ref,
                  q_ref, k_ref, v_ref, gq_ref, gk_ref, gv_ref, la_ref, gr_ref):
    u = _rms(h_ref[...], mg_ref[...]).astype(CDT)
    c, s1, s2 = tab_ref[0], tab_ref[1], tab_ref[2]
    half = MLA_ROPE // 2
    cq = _rms(_dot(u, wp_ref[:, 0:256]), qn_ref[...]).astype(CDT)
    q = _rope128(_dot(cq, wuq_ref[...]), c, s1, s2, LANES - half, half)
    q_ref[...] = (q * (MLA_NOPE + MLA_ROPE) ** -0.5).astype(q_ref.dtype)
    ckv = _rms(_dot(u, wp_ref[:, 256:384]), kvn_ref[...]).astype(CDT)
    krot = _rope128(_dot(u, wp_ref[:, 384:512]), c, s1, s2, LANES - half, half)
    k = _dot(ckv, wuk_ref[...])
    k_ref[...] = (k + jnp.concatenate([krot] * MLA_HEADS, axis=1)).astype(k_ref.dtype)
    v_ref[...] = _dot(ckv, wuv_ref[...]).astype(v_ref.dtype)
    gq_ref[...] = _dot(u, wp_ref[:, 512:768]) * GLA_DK ** -0.5
    gk_ref[...] = _dot(u, wp_ref[:, 768:1024])
    gv_ref[...] = _dot(u, wp_ref[:, 1024:1536])
    z = _dot(_dot(u, wp_ref[:, 1536:1664]).astype(CDT), wa2_ref[...]) + ba_ref[...]
    la_ref[...] = (jnp.minimum(z, 0.0) - jnp.log(1.0 + jnp.exp(-jnp.abs(z)))) * (1.0 / GLA_GATE_TAU)
    gr_ref[...] = _dot(u, wp_ref[:, 1664:2176])


def _hy_prep(h, mix_gain, w_in, q_norm, w_uq, kv_norm, w_ukv, w_a2, b_a, tabs):
    T, D = h.shape
    tm = TM_PROJ
    z = lambda n: jnp.zeros((D, n), F32)
    wp = jnp.concatenate([w_in[:, 0:384], z(64), w_in[:, 384:416], z(32), w_in[:, 416:1440],
                          w_in[:, 1440:1456], z(112), w_in[:, 1456:1968]], axis=1).astype(CDT)
    wuq = jnp.pad(w_uq.reshape(MLA_Q_LORA, MLA_HEADS, MLA_NOPE + MLA_ROPE),
                  ((0, 0), (0, 0), (0, LANES - MLA_NOPE - MLA_ROPE))).reshape(MLA_Q_LORA, MLA_HEADS * LANES).astype(CDT)
    wkv = w_ukv.reshape(MLA_KV_LORA, MLA_HEADS, MLA_NOPE + MLA_V)
    wuk = jnp.pad(wkv[..., :MLA_NOPE], ((0, 0), (0, 0), (0, LANES - MLA_NOPE))).reshape(MLA_KV_LORA, MLA_HEADS * LANES).astype(CDT)
    wuv = wkv[..., MLA_NOPE:].reshape(MLA_KV_LORA, MLA_HEADS * MLA_V).astype(CDT)
    wa2 = jnp.pad(w_a2, ((0, LANES - GLA_GATE_RANK), (0, 0))).astype(CDT)
    row = lambda n, dt=F32: pl.BlockSpec((tm, n), lambda i: (i, 0))
    outs = [(MLA_HEADS * LANES, CDT), (MLA_HEADS * LANES, CDT), (MLA_HEADS * MLA_V, CDT),
            (256, F32), (256, F32), (512, F32), (256, F32), (512, F32)]
    return pl.pallas_call(
        _hy_prep_body,
        grid=(T // tm,),
        in_specs=[row(D), _full((1, D)), _full((D, HY_NP)), _full((1, MLA_Q_LORA)), _full(wuq.shape),
                  _full((1, MLA_KV_LORA)), _full(wuk.shape), _full(wuv.shape), _full(wa2.shape), _full((1, 256)),
                  pl.BlockSpec((3, tm, LANES), lambda i: (0, i, 0))],
        out_specs=[row(n) for n, _ in outs],
        out_shape=[jax.ShapeDtypeStruct((T, n), dt) for n, dt in outs],
        compiler_params=_cparams(("parallel",)),
        name="hy_prep",
    )(h, mix_gain.reshape(1, D), wp, q_norm.reshape(1, -1), wuq, kv_norm.reshape(1, -1), wuk, wuv, wa2,
      b_a.reshape(1, -1), tabs)


def _mla_body(q_ref, k_ref, v_ref, o_ref, m_ref, l_ref, acc_ref, *, tq):
    i = pl.program_id(2)
    row = lax.broadcasted_iota(jnp.int32, (tq, tq), 0)
    col = lax.broadcasted_iota(jnp.int32, (tq, tq), 1)
    causal = row >= col
    outs = []
    for h in range(2):
        q = q_ref[:, h * LANES:(h + 1) * LANES]
        m_ref[...] = jnp.full(m_ref.shape, NEG_INF, F32)
        l_ref[...] = jnp.zeros(l_ref.shape, F32)
        acc_ref[...] = jnp.zeros(acc_ref.shape, F32)

        def tile(j, masked):
            r0 = pl.multiple_of(j * tq, tq)
            k = k_ref[pl.ds(r0, tq), h * LANES:(h + 1) * LANES]
            v = v_ref[pl.ds(r0, tq), :]
            s = _dot_nt(q, k)
            if masked:
                s = jnp.where(causal, s, NEG_INF)
            m_old = m_ref[...]
            m_new = jnp.maximum(m_old, jnp.max(s, axis=-1, keepdims=True))
            p = jnp.exp(s - m_new)
            alpha = jnp.exp(m_old - m_new)
            l_ref[...] = alpha * l_ref[...] + jnp.sum(p, axis=-1, keepdims=True)
            acc_ref[...] = alpha * acc_ref[...] + _dot(p.astype(CDT), v)
            m_ref[...] = m_new

        def body(j, carry):
            tile(j, False)
            return carry

        lax.fori_loop(0, i, body, 0)
        tile(i, True)
        outs.append(acc_ref[...] / jnp.maximum(l_ref[...], TINY))
    lane = lax.broadcasted_iota(jnp.int32, outs[0].shape, 1)
    o_ref[...] = jnp.where(lane < MLA_V, outs[0], outs[1]).astype(o_ref.dtype)


def _mla_attention(q, k, v, B, S):
    tq = TQ_MLA
    nq = S // tq
    return pl.pallas_call(
        functools.partial(_mla_body, tq=tq),
        grid=(B, MLA_HEADS // 2, nq),
        in_specs=[pl.BlockSpec((tq, 2 * LANES), lambda b, hp, i: (b * nq + i, hp)),
                  pl.BlockSpec((S, 2 * LANES), lambda b, hp, i: (b, hp)),
                  pl.BlockSpec((S, 2 * MLA_V), lambda b, hp, i: (b, hp))],
        out_specs=pl.BlockSpec((tq, 2 * MLA_V), lambda b, hp, i: (b * nq + i, hp)),
        out_shape=jax.ShapeDtypeStruct((B * S, MLA_HEADS * MLA_V), CDT),
        scratch_shapes=[pltpu.VMEM((tq, 1), F32), pltpu.VMEM((tq, 1), F32), pltpu.VMEM((tq, LANES), F32)],
        compiler_params=_cparams(("parallel", "parallel", "arbitrary")),
        name="mla_attention",
    )(q, k, v)


def _gla_body(q_ref, k_ref, v_ref, la_ref, r_ref, on_ref, o_ref, st_ref, *, nb):
    C = GLA_CHUNK

    @pl.when(pl.program_id(0) == 0)
    def _():
        st_ref[...] = jnp.zeros(st_ref.shape, F32)

    row = lax.broadcasted_iota(jnp.int32, (C, C), 0)
    col = lax.broadcasted_iota(jnp.int32, (C, C), 1)
    tril = row >= col
    tri = jnp.where(tril, 1.0, 0.0).astype(jnp.bfloat16)
    on = on_ref[...]

    def per_b(b, carry):
        hi, mid, lo = _split3(la_ref[b])
        bc = _dot(tri, hi) + _dot(tri, mid) + _dot(tri, lo)
        bl = bc[C - 1:C, :]
        q, k = q_ref[b], k_ref[b]
        k_dec = (k * jnp.exp(bl - bc)).astype(CDT)
        q_inter = (q * jnp.exp(bc)).astype(CDT)
        q_intra = (q * jnp.exp(bc - bl)).astype(CDT)
        dec = jnp.exp(bl)
        for h in range(GLA_HEADS):
            ks = slice(h * GLA_DK, (h + 1) * GLA_DK)
            vs = slice(h * GLA_DV, (h + 1) * GLA_DV)
            v = v_ref[b, :, vs].astype(CDT)
            st = st_ref[b, h]
            a = jnp.where(tril, _dot_nt(q_intra[:, ks], k_dec[:, ks]), 0.0)
            o = _dot(a.astype(CDT), v) + _dot_nt(q_inter[:, ks], st.astype(CDT))
            st_ref[b, h] = st * dec[:, ks] + _dot_tn(v, k_dec[:, ks])
            o_ref[b, :, vs] = (_rms(o, on) * _silu(r_ref[b, :, vs])).astype(o_ref.dtype)
        return carry

    lax.fori_loop(0, nb, per_b, 0)


def _gla(gq, gk, gv, la, gr, out_norm, B, S):
    C = GLA_CHUNK
    blk = lambda n: pl.BlockSpec((B, C, n), lambda c: (0, c, 0))
    r3 = lambda a: a.reshape(B, S, a.shape[-1])
    return pl.pallas_call(
        functools.partial(_gla_body, nb=B),
        grid=(S // C,),
        in_specs=[blk(256), blk(256), blk(512), blk(256), blk(512), _full((1, GLA_DV))],
        out_specs=blk(512),
        out_shape=jax.ShapeDtypeStruct((B, S, GLA_HEADS * GLA_DV), CDT),
        scratch_shapes=[pltpu.VMEM((B, GLA_HEADS, GLA_DV, GLA_DK), F32)],
        compiler_params=_cparams(("arbitrary",)),
        name="gla",
    )(r3(gq), r3(gk), r3(gv), r3(la), r3(gr), out_norm.reshape(1, GLA_DV)).reshape(B * S, GLA_HEADS * GLA_DV)


def _out_proj_body(*refs, n_in):
    h_ref, xs, ws, o_ref = refs[0], refs[1:1 + n_in], refs[1 + n_in:1 + 2 * n_in], refs[-1]
    y = h_ref[...]
    for x_ref, w_ref in zip(xs, ws):
        y = y + _dot(x_ref[...], w_ref[...])
    o_ref[...] = y


def _out_proj(h, xs, ws):
    T, D = h.shape
    tm = TM_PROJ
    return pl.pallas_call(
        functools.partial(_out_proj_body, n_in=len(xs)),
        grid=(T // tm,),
        in_specs=[pl.BlockSpec((tm, D), lambda i: (i, 0))]
        + [pl.BlockSpec((tm, x.shape[1]), lambda i: (i, 0)) for x in xs] + [_full(w.shape) for w in ws],
        out_specs=pl.BlockSpec((tm, D), lambda i: (i, 0)),
        out_shape=jax.ShapeDtypeStruct((T, D), F32),
        compiler_params=_cparams(("parallel",)),
        name="out_proj",
    )(h, *xs, *[w.astype(CDT) for w in ws])


NSA_NP = 2688


def _nsa_prep_body(h_ref, mg_ref, wp_ref, tab_ref, q_ref, kvc_ref, kvs_ref, kvw_ref, g_ref):
    u = _rms(h_ref[...], mg_ref[...]).astype(CDT)
    c, s1, s2 = tab_ref[0], tab_ref[1], tab_ref[2]
    half = NSA_HEAD_DIM // 2
    q = _rope128(_dot(u, wp_ref[:, 0:1024]), c, s1, s2, LANES - half, half)
    q_ref[...] = (q * NSA_HEAD_DIM ** -0.5).astype(q_ref.dtype)
    kvc_ref[...] = _dot(u, wp_ref[:, 1024:1536]).astype(kvc_ref.dtype)
    lane = lax.broadcasted_iota(jnp.int32, c.shape, 1)
    is_k = lane < NSA_HEAD_DIM
    ck, s1k, s2k = jnp.where(is_k, c, 1.0), jnp.where(is_k, s1, 0.0), jnp.where(is_k, s2, 0.0)
    kvs_ref[...] = _rope128(_dot(u, wp_ref[:, 1536:2048]), ck, s1k, s2k, LANES - half, half).astype(kvs_ref.dtype)
    kvw_ref[...] = _rope128(_dot(u, wp_ref[:, 2048:2560]), ck, s1k, s2k, LANES - half, half).astype(kvw_ref.dtype)
    gates = jax.nn.sigmoid(_dot(u, wp_ref[:, 2560:2688]))
    per_group = 3 * NSA_HG
    g_ref[0] = gates
    for g in range(1, NSA_KV_GROUPS):
        g_ref[g] = pltpu.roll(gates, LANES - per_group * g, 1)


def _nsa_wp(w_in):
    G, Dh = NSA_KV_GROUPS, NSA_HEAD_DIM
    cols = list(range(0, 1024))
    for base in (1024, 1536, 2048):
        for g in range(G):
            cols += list(range(base + g * Dh, base + (g + 1) * Dh))
            cols += list(range(base + 256 + g * Dh, base + 256 + (g + 1) * Dh))
    cols += list(range(2560, 2608))
    wp = w_in[:, np.asarray(cols)]
    return jnp.pad(wp, ((0, 0), (0, NSA_NP - wp.shape[1]))).astype(CDT)


def _nsa_prep(h, mix_gain, w_in, tabs):
    T, D = h.shape
    tm = TM_PROJ
    row = lambda n: pl.BlockSpec((tm, n), lambda i: (i, 0))
    return pl.pallas_call(
        _nsa_prep_body,
        grid=(T // tm,),
        in_specs=[row(D), _full((1, D)), _full((D, NSA_NP)), pl.BlockSpec((3, tm, LANES), lambda i: (1, i, 0))],
        out_specs=[row(1024), row(512), row(512), row(512),
                   pl.BlockSpec((NSA_KV_GROUPS, tm, LANES), lambda i: (0, i, 0))],
        out_shape=[jax.ShapeDtypeStruct((T, 1024), CDT), jax.ShapeDtypeStruct((T, 512), CDT),
                   jax.ShapeDtypeStruct((T, 512), CDT), jax.ShapeDtypeStruct((T, 512), CDT),
                   jax.ShapeDtypeStruct((NSA_KV_GROUPS, T, LANES), F32)],
        compiler_params=_cparams(("parallel",)),
        name="nsa_prep",
    )(h, mix_gain.reshape(1, D), _nsa_wp(w_in), tabs)


def _compress_body(x_ref, wa_ref, wb_ref, pa_ref, pb_ref, w2_ref, tab_ref, o_ref, *, nseg):
    G = NSA_KV_GROUPS
    nl = NSA_CMP_STRIDE
    posb = _dot(pa_ref[...].astype(CDT), wa_ref[...]) + _dot(pb_ref[...].astype(CDT), wb_ref[...])
    c, s1, s2 = tab_ref[0, 0], tab_ref[1, 0], tab_ref[2, 0]
    lane = lax.broadcasted_iota(jnp.int32, c.shape, 1)
    is_k = lane < NSA_HEAD_DIM
    ck, s1k, s2k = jnp.where(is_k, c, 1.0), jnp.where(is_k, s1, 0.0), jnp.where(is_k, s2, 0.0)
    rows = lax.broadcasted_iota(jnp.int32, (nseg, LANES), 0)
    half = NSA_HEAD_DIM // 2
    for g in range(G):
        xg = jnp.concatenate([x_ref[0, :, (l * G + g) * LANES:(l * G + g + 1) * LANES] for l in range(nl)], axis=1)
        ha = _dot(xg, wa_ref[...])
        hb = _dot(xg, wb_ref[...])
        hid = ha + pltpu.roll(hb, nseg - 1, 0) + posb[0:1, :]
        kv = _dot(_silu(hid).astype(CDT), w2_ref[...])
        kv = _rope128(kv, ck, s1k, s2k, LANES - half, half)
        o_ref[0, :, g * LANES:(g + 1) * LANES] = jnp.where(rows < nseg - 1, kv, 0.0).astype(o_ref.dtype)


def _compress(kvc, pos_k, pos_v, ck_w1, ck_w2, cv_w1, cv_w2, tabs_cmp, B, S):
    G, Dh, Hd = NSA_KV_GROUPS, NSA_HEAD_DIM, NSA_CMP_HIDDEN
    nl = NSA_CMP_STRIDE
    nseg = S // nl
    x = kvc.reshape(B, nseg, nl * G * LANES)

    def half_w(lo):
        wk = ck_w1[lo * Dh:(lo + nl) * Dh].reshape(nl, Dh, Hd)
        wv = cv_w1[lo * Dh:(lo + nl) * Dh].reshape(nl, Dh, Hd)
        top = jnp.concatenate([wk, jnp.zeros_like(wk)], axis=2)
        bot = jnp.concatenate([jnp.zeros_like(wv), wv], axis=2)
        return jnp.concatenate([top, bot], axis=1).reshape(nl * 2 * Dh, 2 * Hd).astype(CDT)

    def half_p(lo):
        p = jnp.concatenate([pos_k[lo:lo + nl], pos_v[lo:lo + nl]], axis=1).reshape(1, nl * 2 * Dh)
        return jnp.broadcast_to(p, (8, nl * 2 * Dh))

    zk = jnp.zeros((Hd, Dh), F32)
    w2 = jnp.concatenate([jnp.concatenate([ck_w2, zk], axis=1), jnp.concatenate([zk, cv_w2], axis=1)], axis=0).astype(CDT)
    return pl.pallas_call(
        functools.partial(_compress_body, nseg=nseg),
        grid=(B,),
        in_specs=[pl.BlockSpec((1, nseg, nl * G * LANES), lambda b: (b, 0, 0)),
                  _full((nl * LANES, 2 * Hd)), _full((nl * LANES, 2 * Hd)),
                  _full((8, nl * LANES)), _full((8, nl * LANES)), _full((2 * Hd, LANES)),
                  pl.BlockSpec((3, 1, nseg, LANES), lambda b: (0, b, 0, 0))],
        out_specs=pl.BlockSpec((1, nseg, G * LANES), lambda b: (b, 0, 0)),
        out_shape=jax.ShapeDtypeStruct((B, nseg, G * LANES), CDT),
        compiler_params=_cparams(("parallel",)),
        name="nsa_compress",
    )(x, half_w(0), half_w(nl), half_p(0), half_p(nl), w2, tabs_cmp)


def _nsa_body(q_ref, kvc_ref, kvs_ref, kvw_ref, g_ref, e_ref, ovt_ref, o_ref,
              sc_ref, m_ref, l_ref, acc_ref, *, tq, tk, tkw, ncmp):
    i = pl.program_id(2)
    q0 = i * tq
    Hg, Dh = NSA_HG, NSA_HEAD_DIM
    R = Hg * tq
    nblk = ovt_ref.shape[0]
    zpad = jnp.zeros((tq, LANES - Dh), q_ref.dtype)
    qs = jnp.concatenate([jnp.concatenate([q_ref[:, h * Dh:(h + 1) * Dh], zpad], axis=1) for h in range(Hg)], axis=0)
    trow = q0 + (lax.broadcasted_iota(jnp.int32, (R, 1), 0) & (tq - 1))

    kvc = kvc_ref[0]
    s = _dot_nt(qs, kvc)
    cend = lax.broadcasted_iota(jnp.int32, (1, ncmp), 1) * NSA_CMP_STRIDE + (NSA_CMP_LEN - 1)
    mk = cend <= trow
    s = jnp.where(mk, s, NEG_INF)
    p = jnp.where(mk, jnp.exp(s - jnp.max(s, axis=-1, keepdims=True)), 0.0)
    pn = p / jnp.maximum(jnp.sum(p, axis=-1, keepdims=True), TINY)
    o_c = _dot(pn.astype(CDT), kvc)
    psum = pn[0:tq]
    for h in range(1, Hg):
        psum = psum + pn[h * tq:(h + 1) * tq]

    ovt = ovt_ref[...]
    hi, mid, lo = _split3(psum)
    imp = _dot_nt(ovt, hi) + _dot_nt(ovt, mid) + _dot_nt(ovt, lo)
    blk = lax.broadcasted_iota(jnp.int32, (nblk, tq), 0)
    cur = jnp.right_shift(q0 + lax.broadcasted_iota(jnp.int32, (nblk, tq), 1), 6)
    valid = blk <= cur
    forced = (blk == 0) | (blk == cur) | (blk == cur - 1)
    score = jnp.where(valid, jnp.where(forced, NSA_FORCE_SCORE, imp), NEG_INF)
    sc_ref[...] = score

    def rank_step(mp, rank):
        r = jnp.broadcast_to(sc_ref[pl.ds(mp, 1), :], (nblk, tq))
        tie = jnp.where(blk > mp, 1.0, 0.0)
        return rank + jnp.where(r > score, 1.0, jnp.where(r == score, tie, 0.0))

    rank = lax.fori_loop(0, jnp.right_shift(q0 + tq - 1, 6) + 1, rank_step, jnp.zeros((nblk, tq), F32))
    sel_t = jnp.where(valid, jnp.where(rank < NSA_TOP_N, 1.0, 0.0), 0.0)
    sel = jnp.concatenate([sel_t, jnp.zeros((LANES - nblk, tq), F32)], axis=0).T.astype(CDT)

    def flash_init():
        m_ref[...] = jnp.full(m_ref.shape, NEG_INF, F32)
        l_ref[...] = jnp.zeros(l_ref.shape, F32)
        acc_ref[...] = jnp.zeros(acc_ref.shape, F32)

    def flash_step(s, ok, kv):
        s = jnp.where(ok, s, NEG_INF)
        m_old = m_ref[...]
        m_new = jnp.maximum(m_old, jnp.max(s, axis=-1, keepdims=True))
        p = jnp.where(ok, jnp.exp(s - m_new), 0.0)
        alpha = jnp.exp(m_old - m_new)
        l_ref[...] = alpha * l_ref[...] + jnp.sum(p, axis=-1, keepdims=True)
        acc_ref[...] = alpha * acc_ref[...] + _dot(p.astype(CDT), kv)
        m_ref[...] = m_new

    def flash_out():
        return acc_ref[...] / jnp.maximum(l_ref[...], TINY)

    tcol = q0 + lax.broadcasted_iota(jnp.int32, (tq, 1), 0)

    def sel_tile(j, diag):
        k0 = pl.multiple_of(j * tk, tk)
        kv = kvs_ref[pl.ds(k0, tk), :]
        mf = _dot(sel, e_ref[j])
        if diag:
            kpos = k0 + lax.broadcasted_iota(jnp.int32, (1, tk), 1)
            mf = jnp.where(kpos <= tcol, mf, 0.0)
        ok = jnp.concatenate([mf] * Hg, axis=0) > 0.5
        flash_step(_dot_nt(qs, kv), ok, kv)

    flash_init()
    jd = q0 // tk

    def sel_loop(j, carry):
        sel_tile(j, False)
        return carry

    lax.fori_loop(0, jd, sel_loop, 0)
    sel_tile(jd, True)
    o_s = flash_out()

    flash_init()

    def win_loop(j, carry):
        k0 = pl.multiple_of(j * tkw, tkw)
        kv = kvw_ref[pl.ds(k0, tkw), :]
        d = trow - (k0 + lax.broadcasted_iota(jnp.int32, (1, tkw), 1))
        ok = (d >= 0) & (d < NSA_WINDOW)
        flash_step(_dot_nt(qs, kv), ok, kv)
        return carry

    jw_hi = (q0 + tq - 1) // tkw
    lax.fori_loop(jnp.maximum(q0 - NSA_WINDOW + 1, 0) // tkw, jw_hi + 1, win_loop, 0)
    o_w = flash_out()

    gt = g_ref[0]
    outs = []
    for h in range(Hg):
        rs = slice(h * tq, (h + 1) * tq)
        y = (gt[:, 3 * h:3 * h + 1] * o_c[rs] + gt[:, 3 * h + 1:3 * h + 2] * o_s[rs]
             + gt[:, 3 * h + 2:3 * h + 3] * o_w[rs])
        outs.append(y[:, Dh:])
    o_ref[...] = jnp.concatenate(outs, axis=1).astype(o_ref.dtype)


def _nsa_attention(q, kvc, kvs, kvw, gates, B, S):
    tq, tk, tkw = TQ_NSA, TK_NSA, TKW_NSA
    G = NSA_KV_GROUPS
    nq = S // tq
    ncmp = S // NSA_CMP_STRIDE
    nblk = S // NSA_SEL_LEN
    key_blk = (np.arange(S) // NSA_SEL_LEN).reshape(S // tk, 1, tk)
    e = jnp.asarray(key_blk == np.arange(LANES).reshape(1, LANES, 1), CDT)
    n_cmp = (S - NSA_CMP_LEN) // NSA_CMP_STRIDE + 1
    cs = np.arange(ncmp)[None, :] * NSA_CMP_STRIDE
    bs = np.arange(nblk)[:, None] * NSA_SEL_LEN
    ov = np.clip(np.minimum(cs + NSA_CMP_LEN, bs + NSA_SEL_LEN) - np.maximum(cs, bs), 0, None) / NSA_CMP_LEN
    ov[:, n_cmp:] = 0.0
    ovt = jnp.asarray(ov, jnp.bfloat16)
    R = NSA_HG * tq
    return pl.pallas_call(
        functools.partial(_nsa_body, tq=tq, tk=tk, tkw=tkw, ncmp=ncmp),
        grid=(B, G, nq),
        in_specs=[pl.BlockSpec((tq, NSA_HG * NSA_HEAD_DIM), lambda b, g, i: (b * nq + i, g)),
                  pl.BlockSpec((1, ncmp, LANES), lambda b, g, i: (b, 0, g)),
                  pl.BlockSpec((S, LANES), lambda b, g, i: (b, g)),
                  pl.BlockSpec((S, LANES), lambda b, g, i: (b, g)),
                  pl.BlockSpec((1, tq, LANES), lambda b, g, i: (g, b * nq + i, 0)),
                  _full(e.shape), _full(ovt.shape)],
        out_specs=pl.BlockSpec((tq, NSA_HG * NSA_HEAD_DIM), lambda b, g, i: (b * nq + i, g)),
        out_shape=jax.ShapeDtypeStruct((B * S, NSA_HEADS * NSA_HEAD_DIM), CDT),
        scratch_shapes=[pltpu.VMEM((nblk, tq), F32), pltpu.VMEM((R, 1), F32), pltpu.VMEM((R, 1), F32),
                        pltpu.VMEM((R, LANES), F32)],
        compiler_params=_cparams(("parallel", "parallel", "arbitrary")),
        name="nsa_attention",
    )(q, kvc, kvs, kvw, gates, e, ovt)


def kernel(x, positions, ffn_norm, ffn_w_gate, ffn_w_up, ffn_w_down, mix_norm, hy_w_in, mla_q_norm, mla_w_uq, mla_kv_norm, mla_w_ukv, gla_w_a2, gla_b_a, gla_out_norm, hy_w_out, nsa_w_in, nsa_pos_k, nsa_pos_v, nsa_ck_w1, nsa_ck_w2, nsa_cv_w1, nsa_cv_w2, nsa_w_out, final_norm):
    B, S, D = x.shape
    T = B * S
    depth = ffn_norm.shape[0]
    h = x.reshape(T, D)
    tabs = _rope_tables(positions)
    for layer in range(depth):
        j = layer // 2
        h = _ffn(h, ffn_norm[layer, 0], ffn_w_gate[layer, 0], ffn_w_up[layer, 0], ffn_w_down[layer, 0])
        if layer % 2 == 0:
            q, k, v, gq, gk, gv, la, gr = _hy_prep(h, mix_norm[layer], hy_w_in[j], mla_q_norm[j], mla_w_uq[j],
                                                   mla_kv_norm[j], mla_w_ukv[j], gla_w_a2[j], gla_b_a[j], tabs)
            o_mla = _mla_attention(q, k, v, B, S)
            o_gla = _gla(gq, gk, gv, la, gr, gla_out_norm[j], B, S)
            n_mla = MLA_HEADS * MLA_V
            h = _out_proj(h, [o_mla, o_gla], [hy_w_out[j][:n_mla], hy_w_out[j][n_mla:]])
        else:
            q, kvc, kvs, kvw, gates = _nsa_prep(h, mix_norm[layer], nsa_w_in[j], tabs)
            nseg = S // NSA_CMP_STRIDE
            tc = tabs[3:6].reshape(3, B, nseg, NSA_CMP_STRIDE, LANES)[:, :, 1:, NSA_CMP_STRIDE - 1]
            tabs_cmp = jnp.pad(tc, ((0, 0), (0, 0), (0, 1), (0, 0)))
            kv_cmp = _compress(kvc, nsa_pos_k[j], nsa_pos_v[j], nsa_ck_w1[j], nsa_ck_w2[j], nsa_cv_w1[j],
                               nsa_cv_w2[j], tabs_cmp, B, S)
            o_nsa = _nsa_attention(q, kv_cmp, kvs, kvw, gates, B, S)
            h = _out_proj(h, [o_nsa], [nsa_w_out[j]])
        last = layer == depth - 1
        h = _ffn(h, ffn_norm[layer, 1], ffn_w_gate[layer, 1], ffn_w_up[layer, 1], ffn_w_down[layer, 1],
                 final_gain=final_norm if last else None)
    return h.reshape(B, S, D)
```

```python
import functools

import numpy as np
import jax
import jax.numpy as jnp
from jax import lax
from jax.experimental import pallas as pl
from jax.experimental.pallas import tpu as pltpu

F32 = jnp.float32
CDT = jnp.bfloat16

D_MODEL = 1024
D_FF = 2816
ROPE_THETA = 10000.0
NORM_EPS = 1e-6
NEG_INF = -1e30
TINY = 1e-30
M_FLOOR = -1e29

MLA_HEADS, MLA_NOPE, MLA_ROPE, MLA_V = 8, 64, 32, 64
MLA_Q_LORA, MLA_KV_LORA = 256, 128
GLA_HEADS, GLA_DK, GLA_DV = 4, 64, 128
GLA_GATE_RANK, GLA_GATE_TAU, GLA_CHUNK = 16, 16.0, 64
NSA_HEADS, NSA_KV_GROUPS, NSA_HEAD_DIM = 16, 4, 64
NSA_HG = NSA_HEADS // NSA_KV_GROUPS
NSA_CMP_LEN, NSA_CMP_STRIDE, NSA_CMP_HIDDEN = 32, 16, 128
NSA_SEL_LEN, NSA_TOP_N, NSA_WINDOW = 64, 16, 512
NSA_FORCE_SCORE = 1e4

LANES = 128
VMEM_LIMIT = 52 * 1024 * 1024

TM_FFN = 512
TM_PROJ = 512
TQ_MLA = 512
TQ_NSA = 128
TK_NSA = 256
TKW_NSA = 128
FF_CHUNK = 256


def _cparams(sem):
    return pltpu.CompilerParams(dimension_semantics=sem, vmem_limit_bytes=VMEM_LIMIT)


def _full(shape):
    n = len(shape)
    return pl.BlockSpec(shape, lambda *_: (0,) * n)


def _rms(x, g):
    return x * lax.rsqrt(jnp.mean(x * x, axis=-1, keepdims=True) + NORM_EPS) * g


def _dot(a, b):
    return jnp.dot(a, b, preferred_element_type=F32)


def _dot_nt(a, b):
    return lax.dot_general(a, b, (((1,), (1,)), ((), ())), preferred_element_type=F32)


def _dot_tn(a, b):
    return lax.dot_general(a, b, (((0,), (0,)), ((), ())), preferred_element_type=F32)


def _split3(x):
    hi = x.astype(jnp.bfloat16)
    r = x - hi.astype(F32)
    mid = r.astype(jnp.bfloat16)
    lo = (r - mid.astype(F32)).astype(jnp.bfloat16)
    return hi, mid, lo


def _silu(x):
    return x * jax.nn.sigmoid(x)


def _rope128(x, c, s1, s2, sh_a, sh_b):
    outs = []
    for j in range(x.shape[1] // LANES):
        xs = x[:, j * LANES:(j + 1) * LANES]
        outs.append(xs * c + pltpu.roll(xs, sh_a, 1) * s1 + pltpu.roll(xs, sh_b, 1) * s2)
    return outs[0] if len(outs) == 1 else jnp.concatenate(outs, axis=1)


def _rope_tab_body(pos_ref, pat_ref, o_ref):
    pos = pos_ref[...].astype(F32)
    for t in range(2):
        ang = pos * pat_ref[3 * t:3 * t + 1, :]
        c, s = jnp.cos(ang), jnp.sin(ang)
        o_ref[3 * t] = c
        o_ref[3 * t + 1] = -s * pat_ref[3 * t + 1:3 * t + 2, :]
        o_ref[3 * t + 2] = s * pat_ref[3 * t + 2:3 * t + 3, :]


def _rope_tables(positions):
    T = positions.size
    tm = 1024
    lane = np.arange(LANES)
    pat = np.zeros((8, LANES), np.float32)
    j = np.where((lane >= 64) & (lane < 80), lane - 64, np.where((lane >= 80) & (lane < 96), lane - 80, 0))
    in_rope = (lane >= 64) & (lane < 96)
    pat_mla = jnp.where(in_rope, jnp.power(ROPE_THETA, -jnp.asarray(j, F32) * (2.0 / MLA_ROPE)), 0.0)
    pat_nsa = jnp.power(ROPE_THETA, -jnp.asarray(lane % 32, F32) * (2.0 / NSA_HEAD_DIM))
    pat[1] = (lane >= 64) & (lane < 80)
    pat[2] = (lane >= 80) & (lane < 96)
    pat[4] = (lane % 64) < 32
    pat[5] = (lane % 64) >= 32
    pat = jnp.asarray(pat).at[0].set(pat_mla).at[3].set(pat_nsa)
    return pl.pallas_call(
        _rope_tab_body,
        grid=(T // tm,),
        in_specs=[pl.BlockSpec((tm, 1), lambda i: (i, 0)), _full((8, LANES))],
        out_specs=pl.BlockSpec((6, tm, LANES), lambda i: (0, i, 0)),
        out_shape=jax.ShapeDtypeStruct((6, T, LANES), F32),
        compiler_params=_cparams(("parallel",)),
        name="rope_tables",
    )(positions.reshape(T, 1), pat)


def _ffn_body(x_ref, g_ref, wg_ref, wu_ref, wd_ref, fg_ref, o_ref, *, final):
    x = x_ref[...]
    xn = _rms(x, g_ref[...]).astype(CDT)
    acc = jnp.zeros(x.shape, F32)
    for c in range(D_FF // FF_CHUNK):
        sl = slice(c * FF_CHUNK, (c + 1) * FF_CHUNK)
        g = _dot(xn, wg_ref[:, sl])
        u = _dot(xn, wu_ref[:, sl])
        acc = acc + _dot((_silu(g) * u).astype(CDT), wd_ref[sl, :])
    y = x + 0.5 * acc
    if final:
        y = _rms(y, fg_ref[...])
    o_ref[...] = y


def _ffn(h, gain, wg, wu, wd, final_gain=None):
    T, D = h.shape
    tm = TM_FFN
    final = final_gain is not None
    fg = final_gain if final else gain
    wspec = lambda shape: pl.BlockSpec(shape, lambda i: (0, 0), pipeline_mode=pl.Buffered(1))
    return pl.pallas_call(
        functools.partial(_ffn_body, final=final),
        grid=(T // tm,),
        in_specs=[pl.BlockSpec((tm, D), lambda i: (i, 0)), _full((1, D)),
                  wspec((D, D_FF)), wspec((D, D_FF)), wspec((D_FF, D)), _full((1, D))],
        out_specs=pl.BlockSpec((tm, D), lambda i: (i, 0)),
        out_shape=jax.ShapeDtypeStruct((T, D), F32),
        compiler_params=_cparams(("parallel",)),
        name="ffn_final" if final else "ffn",
    )(h, gain.reshape(1, D), wg.astype(CDT), wu.astype(CDT), wd.astype(CDT), fg.reshape(1, D))


HY_NP = 2176


def _hy_prep_body(h_ref, mg_ref, wp_ref, qn_ref, wuq_ref, kvn_ref, wuk_ref, wuv_ref, wa2_ref, ba_ref, tab_ref,
                  q_ref, k_ref, v_ref, gq_ref, gk_ref, gv_ref, la_ref, gr_ref):
    u = _rms(h_ref[...], mg_ref[...]).astype(CDT)
    c, s1, s2 = tab_ref[0], tab_ref[1], tab_ref[2]
    half = MLA_ROPE // 2
    cq = _rms(_dot(u, wp_ref[:, 0:256]), qn_ref[...]).astype(CDT)
    q = _rope128(_dot(cq, wuq_ref[...]), c, s1, s2, LANES - half, half)
    q_ref[...] = (q * (MLA_NOPE + MLA_ROPE) ** -0.5).astype(q_ref.dtype)
    ckv = _rms(_dot(u, wp_ref[:, 256:384]), kvn_ref[...]).astype(CDT)
    krot = _rope128(_dot(u, wp_ref[:, 384:512]), c, s1, s2, LANES - half, half)
    k = _dot(ckv, wuk_ref[...])
    k_ref[...] = (k + jnp.concatenate([krot] * MLA_HEADS, axis=1)).astype(k_ref.dtype)
    v_ref[...] = _dot(ckv, wuv_ref[...]).astype(v_ref.dtype)
    gq_ref[...] = _dot(u, wp_ref[:, 512:768]) * GLA_DK ** -0.5
    gk_ref[...] = _dot(u, wp_ref[:, 768:1024])
    gv_ref[...] = _dot(u, wp_ref[:, 1024:1536])
    z = _dot(_dot(u, wp_ref[:, 1536:1664]).astype(CDT), wa2_ref[...]) + ba_ref[...]
    la_ref[...] = (jnp.minimum(z, 0.0) - jnp.log(1.0 + jnp.exp(-jnp.abs(z)))) * (1.0 / GLA_GATE_TAU)
    gr_ref[...] = _dot(u, wp_ref[:, 1664:2176])


def _hy_prep(h, mix_gain, w_in, q_norm, w_uq, kv_norm, w_ukv, w_a2, b_a, tabs):
    T, D = h.shape
    tm = TM_PROJ
    z = lambda n: jnp.zeros((D, n), F32)
    wp = jnp.concatenate([w_in[:, 0:384], z(64), w_in[:, 384:416], z(32), w_in[:, 416:1440],
                          w_in[:, 1440:1456], z(112), w_in[:, 1456:1968]], axis=1).astype(CDT)
    wuq = jnp.pad(w_uq.reshape(MLA_Q_LORA, MLA_HEADS, MLA_NOPE + MLA_ROPE),
                  ((0, 0), (0, 0), (0, LANES - MLA_NOPE - MLA_ROPE))).reshape(MLA_Q_LORA, MLA_HEADS * LANES).astype(CDT)
    wkv = w_ukv.reshape(MLA_KV_LORA, MLA_HEADS, MLA_NOPE + MLA_V)
    wuk = jnp.pad(wkv[..., :MLA_NOPE], ((0, 0), (0, 0), (0, LANES - MLA_NOPE))).reshape(MLA_KV_LORA, MLA_HEADS * LANES).astype(CDT)
    wuv = wkv[..., MLA_NOPE:].reshape(MLA_KV_LORA, MLA_HEADS * MLA_V).astype(CDT)
    wa2 = jnp.pad(w_a2, ((0, LANES - GLA_GATE_RANK), (0, 0))).astype(CDT)
    row = lambda n, dt=F32: pl.BlockSpec((tm, n), lambda i: (i, 0))
    outs = [(MLA_HEADS * LANES, CDT), (MLA_HEADS * LANES, CDT), (MLA_HEADS * MLA_V, CDT),
            (256, F32), (256, F32), (512, F32), (256, F32), (512, F32)]
    return pl.pallas_call(
        _hy_prep_body,
        grid=(T // tm,),
        in_specs=[row(D), _full((1, D)), _full((D, HY_NP)), _full((1, MLA_Q_LORA)), _full(wuq.shape),
                  _full((1, MLA_KV_LORA)), _full(wuk.shape), _full(wuv.shape), _full(wa2.shape), _full((1, 256)),
                  pl.BlockSpec((3, tm, LANES), lambda i: (0, i, 0))],
        out_specs=[row(n) for n, _ in outs],
        out_shape=[jax.ShapeDtypeStruct((T, n), dt) for n, dt in outs],
        compiler_params=_cparams(("parallel",)),
        name="hy_prep",
    )(h, mix_gain.reshape(1, D), wp, q_norm.reshape(1, -1), wuq, kv_norm.reshape(1, -1), wuk, wuv, wa2,
      b_a.reshape(1, -1), tabs)


def _mla_body(q_ref, k_ref, v_ref, o_ref, m_ref, acc_ref, *, tq):
    i = pl.program_id(2)
    krow = lax.broadcasted_iota(jnp.int32, (tq, tq), 0)
    qcol = lax.broadcasted_iota(jnp.int32, (tq, tq), 1)
    causal = krow <= qcol
    vlane = lax.broadcasted_iota(jnp.int32, (tq, 2 * MLA_V), 1)
    outs = []
    for h in range(2):
        q = q_ref[:, h * LANES:(h + 1) * LANES]
        own = (vlane >= h * MLA_V) & (vlane < (h + 1) * MLA_V)
        m_ref[...] = jnp.full(m_ref.shape, M_FLOOR, F32)
        acc_ref[...] = jnp.zeros(acc_ref.shape, F32)

        def tile(j, masked):
            r0 = pl.multiple_of(j * tq, tq)
            k = k_ref[pl.ds(r0, tq), h * LANES:(h + 1) * LANES]
            v1 = jnp.where(own, v_ref[pl.ds(r0, tq), :], jnp.ones((), v_ref.dtype))
            s = _dot_nt(k, q)
            if masked:
                s = jnp.where(causal, s, NEG_INF)
            m_old = m_ref[...]
            m_new = jnp.maximum(m_old, jnp.max(s, axis=0, keepdims=True))
            p = jnp.exp(s - m_new)
            acc_ref[...] = jnp.exp(m_old - m_new) * acc_ref[...] + _dot_tn(v1, p.astype(CDT))
            m_ref[...] = m_new

        def body(j, carry):
            tile(j, False)
            return carry

        lax.fori_loop(0, i, body, 0)
        tile(i, True)
        acc = acc_ref[...]
        o_rows = acc[h * MLA_V:(h + 1) * MLA_V]
        l_row = acc[(1 - h) * MLA_V:(1 - h) * MLA_V + 1]
        outs.append(o_rows / jnp.maximum(l_row, TINY))
    o_ref[...] = jnp.concatenate(outs, axis=0).T.astype(o_ref.dtype)


def _mla_attention(q, k, v, B, S):
    tq = TQ_MLA
    nq = S // tq
    return pl.pallas_call(
        functools.partial(_mla_body, tq=tq),
        grid=(B, MLA_HEADS // 2, nq),
        in_specs=[pl.BlockSpec((tq, 2 * LANES), lambda b, hp, i: (b * nq + i, hp)),
                  pl.BlockSpec((S, 2 * LANES), lambda b, hp, i: (b, hp)),
                  pl.BlockSpec((S, 2 * MLA_V), lambda b, hp, i: (b, hp))],
        out_specs=pl.BlockSpec((tq, 2 * MLA_V), lambda b, hp, i: (b * nq + i, hp)),
        out_shape=jax.ShapeDtypeStruct((B * S, MLA_HEADS * MLA_V), CDT),
        scratch_shapes=[pltpu.VMEM((1, tq), F32), pltpu.VMEM((2 * MLA_V, tq), F32)],
        compiler_params=_cparams(("parallel", "parallel", "arbitrary")),
        name="mla_attention",
    )(q, k, v)


def _gla_body(q_ref, k_ref, v_ref, la_ref, r_ref, on_ref, o_ref, st_ref, *, nb):
    C = GLA_CHUNK

    @pl.when(pl.program_id(0) == 0)
    def _():
        st_ref[...] = jnp.zeros(st_ref.shape, F32)

    row = lax.broadcasted_iota(jnp.int32, (C, C), 0)
    col = lax.broadcasted_iota(jnp.int32, (C, C), 1)
    tril = row >= col
    tri = jnp.where(tril, 1.0, 0.0).astype(jnp.bfloat16)
    on = on_ref[...]

    def per_b(b, carry):
        hi, mid, lo = _split3(la_ref[b])
        bc = _dot(tri, hi) + _dot(tri, mid) + _dot(tri, lo)
        bl = bc[C - 1:C, :]
        q, k = q_ref[b], k_ref[b]
        k_dec = (k * jnp.exp(bl - bc)).astype(CDT)
        q_inter = (q * jnp.exp(bc)).astype(CDT)
        q_intra = (q * jnp.exp(bc - bl)).astype(CDT)
        dec = jnp.exp(bl)
        for h in range(GLA_HEADS):
            ks = slice(h * GLA_DK, (h + 1) * GLA_DK)
            vs = slice(h * GLA_DV, (h + 1) * GLA_DV)
            v = v_ref[b, :, vs].astype(CDT)
            st = st_ref[b, h]
            a = jnp.where(tril, _dot_nt(q_intra[:, ks], k_dec[:, ks]), 0.0)
            o = _dot(a.astype(CDT), v) + _dot_nt(q_inter[:, ks], st.astype(CDT))
            st_ref[b, h] = st * dec[:, ks] + _dot_tn(v, k_dec[:, ks])
            o_ref[b, :, vs] = (_rms(o, on) * _silu(r_ref[b, :, vs])).astype(o_ref.dtype)
        return carry

    lax.fori_loop(0, nb, per_b, 0)


def _gla(gq, gk, gv, la, gr, out_norm, B, S):
    C = GLA_CHUNK
    blk = lambda n: pl.BlockSpec((B, C, n), lambda c: (0, c, 0))
    r3 = lambda a: a.reshape(B, S, a.shape[-1])
    return pl.pallas_call(
        functools.partial(_gla_body, nb=B),
        grid=(S // C,),
        in_specs=[blk(256), blk(256), blk(512), blk(256), blk(512), _full((1, GLA_DV))],
        out_specs=blk(512),
        out_shape=jax.ShapeDtypeStruct((B, S, GLA_HEADS * GLA_DV), CDT),
        scratch_shapes=[pltpu.VMEM((B, GLA_HEADS, GLA_DV, GLA_DK), F32)],
        compiler_params=_cparams(("arbitrary",)),
        name="gla",
    )(r3(gq), r3(gk), r3(gv), r3(la), r3(gr), out_norm.reshape(1, GLA_DV)).reshape(B * S, GLA_HEADS * GLA_DV)


def _out_proj_body(*refs, n_in):
    h_ref, xs, ws, o_ref = refs[0], refs[1:1 + n_in], refs[1 + n_in:1 + 2 * n_in], refs[-1]
    y = h_ref[...]
    for x_ref, w_ref in zip(xs, ws):
        y = y + _dot(x_ref[...], w_ref[...])
    o_ref[...] = y


def _out_proj(h, xs, ws):
    T, D = h.shape
    tm = TM_PROJ
    return pl.pallas_call(
        functools.partial(_out_proj_body, n_in=len(xs)),
        grid=(T // tm,),
        in_specs=[pl.BlockSpec((tm, D), lambda i: (i, 0))]
        + [pl.BlockSpec((tm, x.shape[1]), lambda i: (i, 0)) for x in xs] + [_full(w.shape) for w in ws],
        out_specs=pl.BlockSpec((tm, D), lambda i: (i, 0)),
        out_shape=jax.ShapeDtypeStruct((T, D), F32),
        compiler_params=_cparams(("parallel",)),
        name="out_proj",
    )(h, *xs, *[w.astype(CDT) for w in ws])


NSA_NP = 2688


def _nsa_prep_body(h_ref, mg_ref, wp_ref, tab_ref, q_ref, kvc_ref, kvs_ref, kvw_ref, g_ref):
    u = _rms(h_ref[...], mg_ref[...]).astype(CDT)
    c, s1, s2 = tab_ref[0], tab_ref[1], tab_ref[2]
    half = NSA_HEAD_DIM // 2
    q = _rope128(_dot(u, wp_ref[:, 0:1024]), c, s1, s2, LANES - half, half)
    q_ref[...] = (q * NSA_HEAD_DIM ** -0.5).astype(q_ref.dtype)
    kvc_ref[...] = _dot(u, wp_ref[:, 1024:1536]).astype(kvc_ref.dtype)
    lane = lax.broadcasted_iota(jnp.int32, c.shape, 1)
    is_k = lane < NSA_HEAD_DIM
    ck, s1k, s2k = jnp.where(is_k, c, 1.0), jnp.where(is_k, s1, 0.0), jnp.where(is_k, s2, 0.0)
    kvs_ref[...] = _rope128(_dot(u, wp_ref[:, 1536:2048]), ck, s1k, s2k, LANES - half, half).astype(kvs_ref.dtype)
    kvw_ref[...] = _rope128(_dot(u, wp_ref[:, 2048:2560]), ck, s1k, s2k, LANES - half, half).astype(kvw_ref.dtype)
    gates = jax.nn.sigmoid(_dot(u, wp_ref[:, 2560:2688]))
    per_group = 3 * NSA_HG
    g_ref[0] = gates
    for g in range(1, NSA_KV_GROUPS):
        g_ref[g] = pltpu.roll(gates, LANES - per_group * g, 1)


def _nsa_wp(w_in):
    G, Dh = NSA_KV_GROUPS, NSA_HEAD_DIM
    cols = list(range(0, 1024))
    for base in (1024, 1536, 2048):
        for g in range(G):
            cols += list(range(base + g * Dh, base + (g + 1) * Dh))
            cols += list(range(base + 256 + g * Dh, base + 256 + (g + 1) * Dh))
    cols += list(range(2560, 2608))
    wp = w_in[:, np.asarray(cols)]
    return jnp.pad(wp, ((0, 0), (0, NSA_NP - wp.shape[1]))).astype(CDT)


def _nsa_prep(h, mix_gain, w_in, tabs):
    T, D = h.shape
    tm = TM_PROJ
    row = lambda n: pl.BlockSpec((tm, n), lambda i: (i, 0))
    return pl.pallas_call(
        _nsa_prep_body,
        grid=(T // tm,),
        in_specs=[row(D), _full((1, D)), _full((D, NSA_NP)), pl.BlockSpec((3, tm, LANES), lambda i: (1, i, 0))],
        out_specs=[row(1024), row(512), row(512), row(512),
                   pl.BlockSpec((NSA_KV_GROUPS, tm, LANES), lambda i: (0, i, 0))],
        out_shape=[jax.ShapeDtypeStruct((T, 1024), CDT), jax.ShapeDtypeStruct((T, 512), CDT),
                   jax.ShapeDtypeStruct((T, 512), CDT), jax.ShapeDtypeStruct((T, 512), CDT),
                   jax.ShapeDtypeStruct((NSA_KV_GROUPS, T, LANES), F32)],
        compiler_params=_cparams(("parallel",)),
        name="nsa_prep",
    )(h, mix_gain.reshape(1, D), _nsa_wp(w_in), tabs)


def _compress_body(x_ref, wa_ref, wb_ref, pa_ref, pb_ref, w2_ref, tab_ref, o_ref, *, nseg):
    G = NSA_KV_GROUPS
    nl = NSA_CMP_STRIDE
    posb = _dot(pa_ref[...].astype(CDT), wa_ref[...]) + _dot(pb_ref[...].astype(CDT), wb_ref[...])
    c, s1, s2 = tab_ref[0, 0], tab_ref[1, 0], tab_ref[2, 0]
    lane = lax.broadcasted_iota(jnp.int32, c.shape, 1)
    is_k = lane < NSA_HEAD_DIM
    ck, s1k, s2k = jnp.where(is_k, c, 1.0), jnp.where(is_k, s1, 0.0), jnp.where(is_k, s2, 0.0)
    rows = lax.broadcasted_iota(jnp.int32, (nseg, LANES), 0)
    half = NSA_HEAD_DIM // 2
    for g in range(G):
        xg = jnp.concatenate([x_ref[0, :, (l * G + g) * LANES:(l * G + g + 1) * LANES] for l in range(nl)], axis=1)
        ha = _dot(xg, wa_ref[...])
        hb = _dot(xg, wb_ref[...])
        hid = ha + pltpu.roll(hb, nseg - 1, 0) + posb[0:1, :]
        kv = _dot(_silu(hid).astype(CDT), w2_ref[...])
        kv = _rope128(kv, ck, s1k, s2k, LANES - half, half)
        o_ref[0, :, g * LANES:(g + 1) * LANES] = jnp.where(rows < nseg - 1, kv, 0.0).astype(o_ref.dtype)


def _compress(kvc, pos_k, pos_v, ck_w1, ck_w2, cv_w1, cv_w2, tabs_cmp, B, S):
    G, Dh, Hd = NSA_KV_GROUPS, NSA_HEAD_DIM, NSA_CMP_HIDDEN
    nl = NSA_CMP_STRIDE
    nseg = S // nl
    x = kvc.reshape(B, nseg, nl * G * LANES)

    def half_w(lo):
        wk = ck_w1[lo * Dh:(lo + nl) * Dh].reshape(nl, Dh, Hd)
        wv = cv_w1[lo * Dh:(lo + nl) * Dh].reshape(nl, Dh, Hd)
        top = jnp.concatenate([wk, jnp.zeros_like(wk)], axis=2)
        bot = jnp.concatenate([jnp.zeros_like(wv), wv], axis=2)
        return jnp.concatenate([top, bot], axis=1).reshape(nl * 2 * Dh, 2 * Hd).astype(CDT)

    def half_p(lo):
        p = jnp.concatenate([pos_k[lo:lo + nl], pos_v[lo:lo + nl]], axis=1).reshape(1, nl * 2 * Dh)
        return jnp.broadcast_to(p, (8, nl * 2 * Dh))

    zk = jnp.zeros((Hd, Dh), F32)
    w2 = jnp.concatenate([jnp.concatenate([ck_w2, zk], axis=1), jnp.concatenate([zk, cv_w2], axis=1)], axis=0).astype(CDT)
    return pl.pallas_call(
        functools.partial(_compress_body, nseg=nseg),
        grid=(B,),
        in_specs=[pl.BlockSpec((1, nseg, nl * G * LANES), lambda b: (b, 0, 0)),
                  _full((nl * LANES, 2 * Hd)), _full((nl * LANES, 2 * Hd)),
                  _full((8, nl * LANES)), _full((8, nl * LANES)), _full((2 * Hd, LANES)),
                  pl.BlockSpec((3, 1, nseg, LANES), lambda b: (0, b, 0, 0))],
        out_specs=pl.BlockSpec((1, nseg, G * LANES), lambda b: (b, 0, 0)),
        out_shape=jax.ShapeDtypeStruct((B, nseg, G * LANES), CDT),
        compiler_params=_cparams(("parallel",)),
        name="nsa_compress",
    )(x, half_w(0), half_w(nl), half_p(0), half_p(nl), w2, tabs_cmp)


def _nsa_body(q_ref, kvc_ref, kvs_ref, kvw_ref, g_ref, e_ref, ovt_ref, o_ref,
              sc_ref, m_ref, acc_ref, *, tq, tk, tkw, ncmp):
    i = pl.program_id(2)
    q0 = i * tq
    Hg, Dh = NSA_HG, NSA_HEAD_DIM
    R = Hg * tq
    nblk = ovt_ref.shape[0]
    zpad = jnp.zeros((tq, LANES - Dh), q_ref.dtype)
    qs = jnp.concatenate([jnp.concatenate([q_ref[:, h * Dh:(h + 1) * Dh], zpad], axis=1) for h in range(Hg)], axis=0)
    tcol = q0 + lax.broadcasted_iota(jnp.int32, (1, tq), 1)
    tcol_r = q0 + (lax.broadcasted_iota(jnp.int32, (1, R), 1) & (tq - 1))

    def ones_k(kv):
        lane = lax.broadcasted_iota(jnp.int32, kv.shape, 1)
        return jnp.where(lane < Dh, jnp.ones((), kv.dtype), kv)

    kvc = kvc_ref[0]
    s = _dot_nt(kvc, qs)
    cend = lax.broadcasted_iota(jnp.int32, (ncmp, 1), 0) * NSA_CMP_STRIDE + (NSA_CMP_LEN - 1)
    mk = cend <= tcol_r
    s = jnp.where(mk, s, NEG_INF)
    p = jnp.where(mk, jnp.exp(s - jnp.max(s, axis=0, keepdims=True)), 0.0)
    pn = p / jnp.maximum(jnp.sum(p, axis=0, keepdims=True), TINY)
    o_c = _dot_tn(kvc, pn.astype(CDT))[Dh:]
    psum = pn[:, 0:tq]
    for h in range(1, Hg):
        psum = psum + pn[:, h * tq:(h + 1) * tq]

    ovt = ovt_ref[...]
    hi, mid, lo = _split3(psum)
    imp = _dot(ovt, hi) + _dot(ovt, mid) + _dot(ovt, lo)
    blk = lax.broadcasted_iota(jnp.int32, (nblk, tq), 0)
    cur = jnp.right_shift(q0 + lax.broadcasted_iota(jnp.int32, (nblk, tq), 1), 6)
    valid = blk <= cur
    forced = (blk == 0) | (blk == cur) | (blk == cur - 1)
    score = jnp.where(valid, jnp.where(forced, NSA_FORCE_SCORE, imp), NEG_INF)
    sc_ref[...] = score

    def rank_step(mp, rank):
        r = jnp.broadcast_to(sc_ref[pl.ds(mp, 1), :], (nblk, tq))
        tie = jnp.where(blk > mp, 1.0, 0.0)
        return rank + jnp.where(r > score, 1.0, jnp.where(r == score, tie, 0.0))

    rank = lax.fori_loop(0, jnp.right_shift(q0 + tq - 1, 6) + 1, rank_step, jnp.zeros((nblk, tq), F32))
    sel_t = jnp.where(valid, jnp.where(rank < NSA_TOP_N, 1.0, 0.0), 0.0)
    selp = jnp.concatenate([sel_t, jnp.zeros((LANES - nblk, tq), F32)], axis=0).astype(CDT)

    def flash_init():
        m_ref[...] = jnp.full(m_ref.shape, M_FLOOR, F32)
        acc_ref[...] = jnp.zeros(acc_ref.shape, F32)

    def flash_step(s, kv):
        m_old = m_ref[...]
        m_new = jnp.maximum(m_old, jnp.max(s, axis=0, keepdims=True))
        p = jnp.exp(s - m_new)
        acc_ref[...] = jnp.exp(m_old - m_new) * acc_ref[...] + _dot_tn(ones_k(kv), p.astype(CDT))
        m_ref[...] = m_new

    def flash_out():
        acc = acc_ref[...]
        return acc[Dh:] / jnp.maximum(acc[0:1], TINY)

    def sel_tile(j, diag):
        k0 = pl.multiple_of(j * tk, tk)
        kv = kvs_ref[pl.ds(k0, tk), :]
        bias = (_dot(e_ref[j], selp) - 1.0) * (-NEG_INF)
        if diag:
            kpos = k0 + lax.broadcasted_iota(jnp.int32, (tk, 1), 0)
            bias = jnp.where(kpos <= tcol, bias, NEG_INF)
        flash_step(_dot_nt(kv, qs) + jnp.concatenate([bias] * Hg, axis=1), kv)

    flash_init()
    jd = q0 // tk

    def sel_loop(j, carry):
        sel_tile(j, False)
        return carry

    lax.fori_loop(0, jd, sel_loop, 0)
    sel_tile(jd, True)
    o_s = flash_out()

    flash_init()

    def win_loop(j, carry):
        k0 = pl.multiple_of(j * tkw, tkw)
        kv = kvw_ref[pl.ds(k0, tkw), :]
        d = tcol_r - (k0 + lax.broadcasted_iota(jnp.int32, (tkw, 1), 0))
        ok = (d >= 0) & (d < NSA_WINDOW)
        flash_step(jnp.where(ok, _dot_nt(kv, qs), NEG_INF), kv)
        return carry

    lax.fori_loop(jnp.maximum(q0 - NSA_WINDOW + 1, 0) // tkw, (q0 + tq - 1) // tkw + 1, win_loop, 0)
    o_w = flash_out()

    gt = g_ref[0].T
    outs = []
    for h in range(Hg):
        cs = slice(h * tq, (h + 1) * tq)
        outs.append(gt[3 * h:3 * h + 1] * o_c[:, cs] + gt[3 * h + 1:3 * h + 2] * o_s[:, cs]
                    + gt[3 * h + 2:3 * h + 3] * o_w[:, cs])
    o_ref[...] = jnp.concatenate(outs, axis=0).T.astype(o_ref.dtype)


def _nsa_attention(q, kvc, kvs, kvw, gates, B, S):
    tq, tk, tkw = TQ_NSA, TK_NSA, TKW_NSA
    G = NSA_KV_GROUPS
    nq = S // tq
    ncmp = S // NSA_CMP_STRIDE
    nblk = S // NSA_SEL_LEN
    key_blk = (np.arange(S) // NSA_SEL_LEN).reshape(S // tk, tk, 1)
    e = jnp.asarray(key_blk == np.arange(LANES).reshape(1, 1, LANES), CDT)
    n_cmp = (S - NSA_CMP_LEN) // NSA_CMP_STRIDE + 1
    cs = np.arange(ncmp)[None, :] * NSA_CMP_STRIDE
    bs = np.arange(nblk)[:, None] * NSA_SEL_LEN
    ov = np.clip(np.minimum(cs + NSA_CMP_LEN, bs + NSA_SEL_LEN) - np.maximum(cs, bs), 0, None) / NSA_CMP_LEN
    ov[:, n_cmp:] = 0.0
    ovt = jnp.asarray(ov, jnp.bfloat16)
    R = NSA_HG * tq
    return pl.pallas_call(
        functools.partial(_nsa_body, tq=tq, tk=tk, tkw=tkw, ncmp=ncmp),
        grid=(B, G, nq),
        in_specs=[pl.BlockSpec((tq, NSA_HG * NSA_HEAD_DIM), lambda b, g, i: (b * nq + i, g)),
                  pl.BlockSpec((1, ncmp, LANES), lambda b, g, i: (b, 0, g)),
                  pl.BlockSpec((S, LANES), lambda b, g, i: (b, g)),
                  pl.BlockSpec((S, LANES), lambda b, g, i: (b, g)),
                  pl.BlockSpec((1, tq, LANES), lambda b, g, i: (g, b * nq + i, 0)),
                  _full(e.shape), _full(ovt.shape)],
        out_specs=pl.BlockSpec((tq, NSA_HG * NSA_HEAD_DIM), lambda b, g, i: (b * nq + i, g)),
        out_shape=jax.ShapeDtypeStruct((B * S, NSA_HEADS * NSA_HEAD_DIM), CDT),
        scratch_shapes=[pltpu.VMEM((nblk, tq), F32), pltpu.VMEM((1, R), F32), pltpu.VMEM((LANES, R), F32)],
        compiler_params=_cparams(("parallel", "parallel", "arbitrary")),
        name="nsa_attention",
    )(q, kvc, kvs, kvw, gates, e, ovt)


def kernel(x, positions, ffn_norm, ffn_w_gate, ffn_w_up, ffn_w_down, mix_norm, hy_w_in, mla_q_norm, mla_w_uq, mla_kv_norm, mla_w_ukv, gla_w_a2, gla_b_a, gla_out_norm, hy_w_out, nsa_w_in, nsa_pos_k, nsa_pos_v, nsa_ck_w1, nsa_ck_w2, nsa_cv_w1, nsa_cv_w2, nsa_w_out, final_norm):
    B, S, D = x.shape
    T = B * S
    depth = ffn_norm.shape[0]
    h = x.reshape(T, D)
    tabs = _rope_tables(positions)
    for layer in range(depth):
        j = layer // 2
        h = _ffn(h, ffn_norm[layer, 0], ffn_w_gate[layer, 0], ffn_w_up[layer, 0], ffn_w_down[layer, 0])
        if layer % 2 == 0:
            q, k, v, gq, gk, gv, la, gr = _hy_prep(h, mix_norm[layer], hy_w_in[j], mla_q_norm[j], mla_w_uq[j],
                                                   mla_kv_norm[j], mla_w_ukv[j], gla_w_a2[j], gla_b_a[j], tabs)
            o_mla = _mla_attention(q, k, v, B, S)
            o_gla = _gla(gq, gk, gv, la, gr, gla_out_norm[j], B, S)
            n_mla = MLA_HEADS * MLA_V
            h = _out_proj(h, [o_mla, o_gla], [hy_w_out[j][:n_mla], hy_w_out[j][n_mla:]])
        else:
            q, kvc, kvs, kvw, gates = _nsa_prep(h, mix_norm[layer], nsa_w_in[j], tabs)
            nseg = S // NSA_CMP_STRIDE
            tc = tabs[3:6].reshape(3, B, nseg, NSA_CMP_STRIDE, LANES)[:, :, 1:, NSA_CMP_STRIDE - 1]
            tabs_cmp = jnp.pad(tc, ((0, 0), (0, 0), (0, 1), (0, 0)))
            kv_cmp = _compress(kvc, nsa_pos_k[j], nsa_pos_v[j], nsa_ck_w1[j], nsa_ck_w2[j], nsa_cv_w1[j],
                               nsa_cv_w2[j], tabs_cmp, B, S)
            o_nsa = _nsa_attention(q, kv_cmp, kvs, kvw, gates, B, S)
            h = _out_proj(h, [o_nsa], [nsa_w_out[j]])
        last = layer == depth - 1
        h = _ffn(h, ffn_norm[layer, 1], ffn_w_gate[layer, 1], ffn_w_up[layer, 1], ffn_w_down[layer, 1],
                 final_gain=final_norm if last else None)
    return h.reshape(B, S, D)
```

```python
import functools

import numpy as np
import jax
import jax.numpy as jnp
from jax import lax
from jax.experimental import pallas as pl
from jax.experimental.pallas import tpu as pltpu

F32 = jnp.float32
CDT = jnp.bfloat16

D_MODEL = 1024
D_FF = 2816
ROPE_THETA = 10000.0
NORM_EPS = 1e-6
NEG_INF = -1e30
TINY = 1e-30
M_FLOOR = -1e29

MLA_HEADS, MLA_NOPE, MLA_ROPE, MLA_V = 8, 64, 32, 64
MLA_Q_LORA, MLA_KV_LORA = 256, 128
GLA_HEADS, GLA_DK, GLA_DV = 4, 64, 128
GLA_GATE_RANK, GLA_GATE_TAU, GLA_CHUNK = 16, 16.0, 64
NSA_HEADS, NSA_KV_GROUPS, NSA_HEAD_DIM = 16, 4, 64
NSA_HG = NSA_HEADS // NSA_KV_GROUPS
NSA_CMP_LEN, NSA_CMP_STRIDE, NSA_CMP_HIDDEN = 32, 16, 128
NSA_SEL_LEN, NSA_TOP_N, NSA_WINDOW = 64, 16, 512
NSA_FORCE_SCORE = 1e4

LANES = 128
VMEM_LIMIT = 52 * 1024 * 1024

TM_FFN = 512
TM_PROJ = 512
TQ_MLA = 512
TQ_NSA = 256
TK_NSA = 512
TKW_NSA = NSA_WINDOW + TQ_NSA
FF_CHUNK = 256


def _cparams(sem):
    return pltpu.CompilerParams(dimension_semantics=sem, vmem_limit_bytes=VMEM_LIMIT)


def _full(shape):
    n = len(shape)
    return pl.BlockSpec(shape, lambda *_: (0,) * n)


def _rms(x, g):
    return x * lax.rsqrt(jnp.mean(x * x, axis=-1, keepdims=True) + NORM_EPS) * g


def _dot(a, b):
    return jnp.dot(a, b, preferred_element_type=F32)


def _dot_nt(a, b):
    return lax.dot_general(a, b, (((1,), (1,)), ((), ())), preferred_element_type=F32)


def _dot_tn(a, b):
    return lax.dot_general(a, b, (((0,), (0,)), ((), ())), preferred_element_type=F32)


def _split3(x):
    hi = x.astype(jnp.bfloat16)
    r = x - hi.astype(F32)
    mid = r.astype(jnp.bfloat16)
    lo = (r - mid.astype(F32)).astype(jnp.bfloat16)
    return hi, mid, lo


def _silu(x):
    return x * jax.nn.sigmoid(x)


def _rope128(x, c, s1, s2, sh_a, sh_b):
    outs = []
    for j in range(x.shape[1] // LANES):
        xs = x[:, j * LANES:(j + 1) * LANES]
        outs.append(xs * c + pltpu.roll(xs, sh_a, 1) * s1 + pltpu.roll(xs, sh_b, 1) * s2)
    return outs[0] if len(outs) == 1 else jnp.concatenate(outs, axis=1)


def _rope_tab_body(pos_ref, pat_ref, o_ref):
    pos = pos_ref[...].astype(F32)
    for t in range(2):
        ang = pos * pat_ref[3 * t:3 * t + 1, :]
        c, s = jnp.cos(ang), jnp.sin(ang)
        o_ref[3 * t] = c
        o_ref[3 * t + 1] = -s * pat_ref[3 * t + 1:3 * t + 2, :]
        o_ref[3 * t + 2] = s * pat_ref[3 * t + 2:3 * t + 3, :]


def _rope_tables(positions):
    T = positions.size
    tm = 1024
    lane = np.arange(LANES)
    pat = np.zeros((8, LANES), np.float32)
    j = np.where((lane >= 64) & (lane < 80), lane - 64, np.where((lane >= 80) & (lane < 96), lane - 80, 0))
    in_rope = (lane >= 64) & (lane < 96)
    pat_mla = jnp.where(in_rope, jnp.power(ROPE_THETA, -jnp.asarray(j, F32) * (2.0 / MLA_ROPE)), 0.0)
    pat_nsa = jnp.power(ROPE_THETA, -jnp.asarray(lane % 32, F32) * (2.0 / NSA_HEAD_DIM))
    pat[1] = (lane >= 64) & (lane < 80)
    pat[2] = (lane >= 80) & (lane < 96)
    pat[4] = (lane % 64) < 32
    pat[5] = (lane % 64) >= 32
    pat = jnp.asarray(pat).at[0].set(pat_mla).at[3].set(pat_nsa)
    return pl.pallas_call(
        _rope_tab_body,
        grid=(T // tm,),
        in_specs=[pl.BlockSpec((tm, 1), lambda i: (i, 0)), _full((8, LANES))],
        out_specs=pl.BlockSpec((6, tm, LANES), lambda i: (0, i, 0)),
        out_shape=jax.ShapeDtypeStruct((6, T, LANES), F32),
        compiler_params=_cparams(("parallel",)),
        name="rope_tables",
    )(positions.reshape(T, 1), pat)


def _ffn_body(x_ref, g_ref, wg_ref, wu_ref, wd_ref, fg_ref, o_ref, *, final):
    x = x_ref[...]
    xn = _rms(x, g_ref[...]).astype(CDT)
    acc = jnp.zeros(x.shape, F32)
    for c in range(D_FF // FF_CHUNK):
        sl = slice(c * FF_CHUNK, (c + 1) * FF_CHUNK)
        g = _dot(xn, wg_ref[:, sl])
        u = _dot(xn, wu_ref[:, sl])
        acc = acc + _dot((_silu(g) * u).astype(CDT), wd_ref[sl, :])
    y = x + 0.5 * acc
    if final:
        y = _rms(y, fg_ref[...])
    o_ref[...] = y


def _ffn(h, gain, wg, wu, wd, final_gain=None):
    T, D = h.shape
    tm = TM_FFN
    final = final_gain is not None
    fg = final_gain if final else gain
    wspec = lambda shape: pl.BlockSpec(shape, lambda i: (0, 0), pipeline_mode=pl.Buffered(1))
    return pl.pallas_call(
        functools.partial(_ffn_body, final=final),
        grid=(T // tm,),
        in_specs=[pl.BlockSpec((tm, D), lambda i: (i, 0)), _full((1, D)),
                  wspec((D, D_FF)), wspec((D, D_FF)), wspec((D_FF, D)), _full((1, D))],
        out_specs=pl.BlockSpec((tm, D), lambda i: (i, 0)),
        out_shape=jax.ShapeDtypeStruct((T, D), F32),
        compiler_params=_cparams(("parallel",)),
        name="ffn_final" if final else "ffn",
    )(h, gain.reshape(1, D), wg.astype(CDT), wu.astype(CDT), wd.astype(CDT), fg.reshape(1, D))


HY_NP = 2176


def _hy_prep_body(h_ref, mg_ref, wp_ref, qn_ref, wuq_ref, kvn_ref, wuk_ref, wuv_ref, wa2_ref, ba_ref, tab_ref,
                  q_ref, k_ref, v_ref, gq_ref, gk_ref, gv_ref, la_ref, gr_ref):
    u = _rms(h_ref[...], mg_ref[...]).astype(CDT)
    c, s1, s2 = tab_ref[0], tab_ref[1], tab_ref[2]
    half = MLA_ROPE // 2
    cq = _rms(_dot(u, wp_ref[:, 0:256]), qn_ref[...]).astype(CDT)
    q = _rope128(_dot(cq, wuq_ref[...]), c, s1, s2, LANES - half, half)
    q_ref[...] = (q * (MLA_NOPE + MLA_ROPE) ** -0.5).astype(q_ref.dtype)
    ckv = _rms(_dot(u, wp_ref[:, 256:384]), kvn_ref[...]).astype(CDT)
    krot = _rope128(_dot(u, wp_ref[:, 384:512]), c, s1, s2, LANES - half, half)
    k = _dot(ckv, wuk_ref[...])
    k_ref[...] = (k + jnp.concatenate([krot] * MLA_HEADS, axis=1)).astype(k_ref.dtype)
    v_ref[...] = _dot(ckv, wuv_ref[...]).astype(v_ref.dtype)
    gq_ref[...] = _dot(u, wp_ref[:, 512:768]) * GLA_DK ** -0.5
    gk_ref[...] = _dot(u, wp_ref[:, 768:1024])
    gv_ref[...] = _dot(u, wp_ref[:, 1024:1536])
    z = _dot(_dot(u, wp_ref[:, 1536:1664]).astype(CDT), wa2_ref[...]) + ba_ref[...]
    la_ref[...] = (jnp.minimum(z, 0.0) - jnp.log(1.0 + jnp.exp(-jnp.abs(z)))) * (1.0 / GLA_GATE_TAU)
    gr_ref[...] = _dot(u, wp_ref[:, 1664:2176])


def _hy_prep(h, mix_gain, w_in, q_norm, w_uq, kv_norm, w_ukv, w_a2, b_a, tabs):
    T, D = h.shape
    tm = TM_PROJ
    z = lambda n: jnp.zeros((D, n), F32)
    wp = jnp.concatenate([w_in[:, 0:384], z(64), w_in[:, 384:416], z(32), w_in[:, 416:1440],
                          w_in[:, 1440:1456], z(112), w_in[:, 1456:1968]], axis=1).astype(CDT)
    wuq = jnp.pad(w_uq.reshape(MLA_Q_LORA, MLA_HEADS, MLA_NOPE + MLA_ROPE),
                  ((0, 0), (0, 0), (0, LANES - MLA_NOPE - MLA_ROPE))).reshape(MLA_Q_LORA, MLA_HEADS * LANES).astype(CDT)
    wkv = w_ukv.reshape(MLA_KV_LORA, MLA_HEADS, MLA_NOPE + MLA_V)
    wuk = jnp.pad(wkv[..., :MLA_NOPE], ((0, 0), (0, 0), (0, LANES - MLA_NOPE))).reshape(MLA_KV_LORA, MLA_HEADS * LANES).astype(CDT)
    wuv = wkv[..., MLA_NOPE:].reshape(MLA_KV_LORA, MLA_HEADS * MLA_V).astype(CDT)
    wa2 = jnp.pad(w_a2, ((0, LANES - GLA_GATE_RANK), (0, 0))).astype(CDT)
    row = lambda n, dt=F32: pl.BlockSpec((tm, n), lambda i: (i, 0))
    outs = [(MLA_HEADS * LANES, CDT), (MLA_HEADS * LANES, CDT), (MLA_HEADS * MLA_V, CDT),
            (256, F32), (256, F32), (512, F32), (256, F32), (512, F32)]
    return pl.pallas_call(
        _hy_prep_body,
        grid=(T // tm,),
        in_specs=[row(D), _full((1, D)), _full((D, HY_NP)), _full((1, MLA_Q_LORA)), _full(wuq.shape),
                  _full((1, MLA_KV_LORA)), _full(wuk.shape), _full(wuv.shape), _full(wa2.shape), _full((1, 256)),
                  pl.BlockSpec((3, tm, LANES), lambda i: (0, i, 0))],
        out_specs=[row(n) for n, _ in outs],
        out_shape=[jax.ShapeDtypeStruct((T, n), dt) for n, dt in outs],
        compiler_params=_cparams(("parallel",)),
        name="hy_prep",
    )(h, mix_gain.reshape(1, D), wp, q_norm.reshape(1, -1), wuq, kv_norm.reshape(1, -1), wuk, wuv, wa2,
      b_a.reshape(1, -1), tabs)


def _mla_body(q_ref, k_ref, v_ref, o_ref, m_ref, acc_ref, *, tq):
    i = pl.program_id(2)
    krow = lax.broadcasted_iota(jnp.int32, (tq, tq), 0)
    qcol = lax.broadcasted_iota(jnp.int32, (tq, tq), 1)
    causal = krow <= qcol
    vlane = lax.broadcasted_iota(jnp.int32, (tq, 2 * MLA_V), 1)
    outs = []
    for h in range(2):
        q = q_ref[:, h * LANES:(h + 1) * LANES]
        own = (vlane >= h * MLA_V) & (vlane < (h + 1) * MLA_V)
        m_ref[...] = jnp.full(m_ref.shape, M_FLOOR, F32)
        acc_ref[...] = jnp.zeros(acc_ref.shape, F32)

        def tile(j, masked):
            r0 = pl.multiple_of(j * tq, tq)
            k = k_ref[pl.ds(r0, tq), h * LANES:(h + 1) * LANES]
            v1 = jnp.where(own, v_ref[pl.ds(r0, tq), :], jnp.ones((), v_ref.dtype))
            s = _dot_nt(k, q)
            if masked:
                s = jnp.where(causal, s, NEG_INF)
            m_old = m_ref[...]
            m_new = jnp.maximum(m_old, jnp.max(s, axis=0, keepdims=True))
            p = jnp.exp(s - m_new)
            acc_ref[...] = jnp.exp(m_old - m_new) * acc_ref[...] + _dot_tn(v1, p.astype(CDT))
            m_ref[...] = m_new

        def body(j, carry):
            tile(j, False)
            return carry

        lax.fori_loop(0, i, body, 0)
        tile(i, True)
        acc = acc_ref[...]
        o_rows = acc[h * MLA_V:(h + 1) * MLA_V]
        l_row = acc[(1 - h) * MLA_V:(1 - h) * MLA_V + 1]
        outs.append(o_rows / jnp.maximum(l_row, TINY))
    o_ref[...] = jnp.concatenate(outs, axis=0).T.astype(o_ref.dtype)


def _mla_attention(q, k, v, B, S):
    tq = TQ_MLA
    nq = S // tq
    return pl.pallas_call(
        functools.partial(_mla_body, tq=tq),
        grid=(B, MLA_HEADS // 2, nq),
        in_specs=[pl.BlockSpec((tq, 2 * LANES), lambda b, hp, i: (b * nq + i, hp)),
                  pl.BlockSpec((S, 2 * LANES), lambda b, hp, i: (b, hp)),
                  pl.BlockSpec((S, 2 * MLA_V), lambda b, hp, i: (b, hp))],
        out_specs=pl.BlockSpec((tq, 2 * MLA_V), lambda b, hp, i: (b * nq + i, hp)),
        out_shape=jax.ShapeDtypeStruct((B * S, MLA_HEADS * MLA_V), CDT),
        scratch_shapes=[pltpu.VMEM((1, tq), F32), pltpu.VMEM((2 * MLA_V, tq), F32)],
        compiler_params=_cparams(("parallel", "parallel", "arbitrary")),
        name="mla_attention",
    )(q, k, v)


def _gla_body(q_ref, k_ref, v_ref, la_ref, r_ref, on_ref, o_ref, st_ref, *, nb):
    C = GLA_CHUNK

    @pl.when(pl.program_id(0) == 0)
    def _():
        st_ref[...] = jnp.zeros(st_ref.shape, F32)

    row = lax.broadcasted_iota(jnp.int32, (C, C), 0)
    col = lax.broadcasted_iota(jnp.int32, (C, C), 1)
    tril = row >= col
    tri = jnp.where(tril, 1.0, 0.0).astype(jnp.bfloat16)
    on = on_ref[...]

    def per_b(b, carry):
        hi, mid, lo = _split3(la_ref[b])
        bc = _dot(tri, hi) + _dot(tri, mid) + _dot(tri, lo)
        bl = bc[C - 1:C, :]
        q, k = q_ref[b], k_ref[b]
        k_dec = (k * jnp.exp(bl - bc)).astype(CDT)
        q_inter = (q * jnp.exp(bc)).astype(CDT)
        q_intra = (q * jnp.exp(bc - bl)).astype(CDT)
        dec = jnp.exp(bl)
        for h in range(GLA_HEADS):
            ks = slice(h * GLA_DK, (h + 1) * GLA_DK)
            vs = slice(h * GLA_DV, (h + 1) * GLA_DV)
            v = v_ref[b, :, vs].astype(CDT)
            st = st_ref[b, h]
            a = jnp.where(tril, _dot_nt(q_intra[:, ks], k_dec[:, ks]), 0.0)
            o = _dot(a.astype(CDT), v) + _dot_nt(q_inter[:, ks], st.astype(CDT))
            st_ref[b, h] = st * dec[:, ks] + _dot_tn(v, k_dec[:, ks])
            o_ref[b, :, vs] = (_rms(o, on) * _silu(r_ref[b, :, vs])).astype(o_ref.dtype)
        return carry

    lax.fori_loop(0, nb, per_b, 0)


def _gla(gq, gk, gv, la, gr, out_norm, B, S):
    C = GLA_CHUNK
    blk = lambda n: pl.BlockSpec((B, C, n), lambda c: (0, c, 0))
    r3 = lambda a: a.reshape(B, S, a.shape[-1])
    return pl.pallas_call(
        functools.partial(_gla_body, nb=B),
        grid=(S // C,),
        in_specs=[blk(256), blk(256), blk(512), blk(256), blk(512), _full((1, GLA_DV))],
        out_specs=blk(512),
        out_shape=jax.ShapeDtypeStruct((B, S, GLA_HEADS * GLA_DV), CDT),
        scratch_shapes=[pltpu.VMEM((B, GLA_HEADS, GLA_DV, GLA_DK), F32)],
        compiler_params=_cparams(("arbitrary",)),
        name="gla",
    )(r3(gq), r3(gk), r3(gv), r3(la), r3(gr), out_norm.reshape(1, GLA_DV)).reshape(B * S, GLA_HEADS * GLA_DV)


def _out_proj_body(*refs, n_in):
    h_ref, xs, ws, o_ref = refs[0], refs[1:1 + n_in], refs[1 + n_in:1 + 2 * n_in], refs[-1]
    y = h_ref[...]
    for x_ref, w_ref in zip(xs, ws):
        y = y + _dot(x_ref[...], w_ref[...])
    o_ref[...] = y


def _out_proj(h, xs, ws):
    T, D = h.shape
    tm = TM_PROJ
    return pl.pallas_call(
        functools.partial(_out_proj_body, n_in=len(xs)),
        grid=(T // tm,),
        in_specs=[pl.BlockSpec((tm, D), lambda i: (i, 0))]
        + [pl.BlockSpec((tm, x.shape[1]), lambda i: (i, 0)) for x in xs] + [_full(w.shape) for w in ws],
        out_specs=pl.BlockSpec((tm, D), lambda i: (i, 0)),
        out_shape=jax.ShapeDtypeStruct((T, D), F32),
        compiler_params=_cparams(("parallel",)),
        name="out_proj",
    )(h, *xs, *[w.astype(CDT) for w in ws])


NSA_NP = 2688


def _nsa_prep_body(h_ref, mg_ref, wp_ref, tab_ref, q_ref, kvc_ref, kvs_ref, kvw_ref, g_ref):
    u = _rms(h_ref[...], mg_ref[...]).astype(CDT)
    c, s1, s2 = tab_ref[0], tab_ref[1], tab_ref[2]
    half = NSA_HEAD_DIM // 2
    q = _rope128(_dot(u, wp_ref[:, 0:1024]), c, s1, s2, LANES - half, half)
    q_ref[...] = (q * NSA_HEAD_DIM ** -0.5).astype(q_ref.dtype)
    kvc_ref[...] = _dot(u, wp_ref[:, 1024:1536]).astype(kvc_ref.dtype)
    lane = lax.broadcasted_iota(jnp.int32, c.shape, 1)
    is_k = lane < NSA_HEAD_DIM
    ck, s1k, s2k = jnp.where(is_k, c, 1.0), jnp.where(is_k, s1, 0.0), jnp.where(is_k, s2, 0.0)
    kvs_ref[...] = _rope128(_dot(u, wp_ref[:, 1536:2048]), ck, s1k, s2k, LANES - half, half).astype(kvs_ref.dtype)
    kvw_ref[...] = _rope128(_dot(u, wp_ref[:, 2048:2560]), ck, s1k, s2k, LANES - half, half).astype(kvw_ref.dtype)
    gates = jax.nn.sigmoid(_dot(u, wp_ref[:, 2560:2688]))
    per_group = 3 * NSA_HG
    g_ref[0] = gates
    for g in range(1, NSA_KV_GROUPS):
        g_ref[g] = pltpu.roll(gates, LANES - per_group * g, 1)


def _nsa_wp(w_in):
    G, Dh = NSA_KV_GROUPS, NSA_HEAD_DIM
    cols = list(range(0, 1024))
    for base in (1024, 1536, 2048):
        for g in range(G):
            cols += list(range(base + g * Dh, base + (g + 1) * Dh))
            cols += list(range(base + 256 + g * Dh, base + 256 + (g + 1) * Dh))
    cols += list(range(2560, 2608))
    wp = w_in[:, np.asarray(cols)]
    return jnp.pad(wp, ((0, 0), (0, NSA_NP - wp.shape[1]))).astype(CDT)


def _nsa_prep(h, mix_gain, w_in, tabs):
    T, D = h.shape
    tm = TM_PROJ
    row = lambda n: pl.BlockSpec((tm, n), lambda i: (i, 0))
    return pl.pallas_call(
        _nsa_prep_body,
        grid=(T // tm,),
        in_specs=[row(D), _full((1, D)), _full((D, NSA_NP)), pl.BlockSpec((3, tm, LANES), lambda i: (1, i, 0))],
        out_specs=[row(1024), row(512), row(512), row(512),
                   pl.BlockSpec((NSA_KV_GROUPS, tm, LANES), lambda i: (0, i, 0))],
        out_shape=[jax.ShapeDtypeStruct((T, 1024), CDT), jax.ShapeDtypeStruct((T, 512), CDT),
                   jax.ShapeDtypeStruct((T, 512), CDT), jax.ShapeDtypeStruct((T, 512), CDT),
                   jax.ShapeDtypeStruct((NSA_KV_GROUPS, T, LANES), F32)],
        compiler_params=_cparams(("parallel",)),
        name="nsa_prep",
    )(h, mix_gain.reshape(1, D), _nsa_wp(w_in), tabs)


def _compress_body(x_ref, wa_ref, wb_ref, pa_ref, pb_ref, w2_ref, tab_ref, o_ref, *, nseg):
    G = NSA_KV_GROUPS
    nl = NSA_CMP_STRIDE
    posb = _dot(pa_ref[...].astype(CDT), wa_ref[...]) + _dot(pb_ref[...].astype(CDT), wb_ref[...])
    c, s1, s2 = tab_ref[0, 0], tab_ref[1, 0], tab_ref[2, 0]
    lane = lax.broadcasted_iota(jnp.int32, c.shape, 1)
    is_k = lane < NSA_HEAD_DIM
    ck, s1k, s2k = jnp.where(is_k, c, 1.0), jnp.where(is_k, s1, 0.0), jnp.where(is_k, s2, 0.0)
    rows = lax.broadcasted_iota(jnp.int32, (nseg, LANES), 0)
    half = NSA_HEAD_DIM // 2
    for g in range(G):
        xg = jnp.concatenate([x_ref[0, :, (l * G + g) * LANES:(l * G + g + 1) * LANES] for l in range(nl)], axis=1)
        ha = _dot(xg, wa_ref[...])
        hb = _dot(xg, wb_ref[...])
        hid = ha + pltpu.roll(hb, nseg - 1, 0) + posb[0:1, :]
        kv = _dot(_silu(hid).astype(CDT), w2_ref[...])
        kv = _rope128(kv, ck, s1k, s2k, LANES - half, half)
        o_ref[0, :, g * LANES:(g + 1) * LANES] = jnp.where(rows < nseg - 1, kv, 0.0).astype(o_ref.dtype)


def _compress(kvc, pos_k, pos_v, ck_w1, ck_w2, cv_w1, cv_w2, tabs_cmp, B, S):
    G, Dh, Hd = NSA_KV_GROUPS, NSA_HEAD_DIM, NSA_CMP_HIDDEN
    nl = NSA_CMP_STRIDE
    nseg = S // nl
    x = kvc.reshape(B, nseg, nl * G * LANES)

    def half_w(lo):
        wk = ck_w1[lo * Dh:(lo + nl) * Dh].reshape(nl, Dh, Hd)
        wv = cv_w1[lo * Dh:(lo + nl) * Dh].reshape(nl, Dh, Hd)
        top = jnp.concatenate([wk, jnp.zeros_like(wk)], axis=2)
        bot = jnp.concatenate([jnp.zeros_like(wv), wv], axis=2)
        return jnp.concatenate([top, bot], axis=1).reshape(nl * 2 * Dh, 2 * Hd).astype(CDT)

    def half_p(lo):
        p = jnp.concatenate([pos_k[lo:lo + nl], pos_v[lo:lo + nl]], axis=1).reshape(1, nl * 2 * Dh)
        return jnp.broadcast_to(p, (8, nl * 2 * Dh))

    zk = jnp.zeros((Hd, Dh), F32)
    w2 = jnp.concatenate([jnp.concatenate([ck_w2, zk], axis=1), jnp.concatenate([zk, cv_w2], axis=1)], axis=0).astype(CDT)
    return pl.pallas_call(
        functools.partial(_compress_body, nseg=nseg),
        grid=(B,),
        in_specs=[pl.BlockSpec((1, nseg, nl * G * LANES), lambda b: (b, 0, 0)),
                  _full((nl * LANES, 2 * Hd)), _full((nl * LANES, 2 * Hd)),
                  _full((8, nl * LANES)), _full((8, nl * LANES)), _full((2 * Hd, LANES)),
                  pl.BlockSpec((3, 1, nseg, LANES), lambda b: (0, b, 0, 0))],
        out_specs=pl.BlockSpec((1, nseg, G * LANES), lambda b: (b, 0, 0)),
        out_shape=jax.ShapeDtypeStruct((B, nseg, G * LANES), CDT),
        compiler_params=_cparams(("parallel",)),
        name="nsa_compress",
    )(x, half_w(0), half_w(nl), half_p(0), half_p(nl), w2, tabs_cmp)


def _nsa_body(q_ref, kvc_ref, kvs_ref, kvw_ref, g_ref, e_ref, ovt_ref, o_ref,
              sc_ref, m_ref, acc_ref, *, tq, tk, tkw, ncmp):
    i = pl.program_id(2)
    q0 = i * tq
    Hg, Dh = NSA_HG, NSA_HEAD_DIM
    R = Hg * tq
    nblk = ovt_ref.shape[0]
    zpad = jnp.zeros((tq, LANES - Dh), q_ref.dtype)
    qs = jnp.concatenate([jnp.concatenate([q_ref[:, h * Dh:(h + 1) * Dh], zpad], axis=1) for h in range(Hg)], axis=0)
    tcol = q0 + lax.broadcasted_iota(jnp.int32, (1, tq), 1)
    tcol_r = q0 + (lax.broadcasted_iota(jnp.int32, (1, R), 1) & (tq - 1))

    def ones_k(kv):
        lane = lax.broadcasted_iota(jnp.int32, kv.shape, 1)
        return jnp.where(lane < Dh, jnp.ones((), kv.dtype), kv)

    kvc = kvc_ref[0]
    s = _dot_nt(kvc, qs)
    cend = lax.broadcasted_iota(jnp.int32, (ncmp, 1), 0) * NSA_CMP_STRIDE + (NSA_CMP_LEN - 1)
    mk = cend <= tcol_r
    s = jnp.where(mk, s, NEG_INF)
    p = jnp.where(mk, jnp.exp(s - jnp.max(s, axis=0, keepdims=True)), 0.0)
    pn = p / jnp.maximum(jnp.sum(p, axis=0, keepdims=True), TINY)
    o_c = _dot_tn(kvc, pn.astype(CDT))[Dh:]
    psum = pn[:, 0:tq]
    for h in range(1, Hg):
        psum = psum + pn[:, h * tq:(h + 1) * tq]

    ovt = ovt_ref[...]
    hi, mid, lo = _split3(psum)
    imp = _dot(ovt, hi) + _dot(ovt, mid) + _dot(ovt, lo)
    blk = lax.broadcasted_iota(jnp.int32, (nblk, tq), 0)
    cur = jnp.right_shift(q0 + lax.broadcasted_iota(jnp.int32, (nblk, tq), 1), 6)
    valid = blk <= cur
    forced = (blk == 0) | (blk == cur) | (blk == cur - 1)
    score = jnp.where(valid, jnp.where(forced, NSA_FORCE_SCORE, imp), NEG_INF)
    sc_ref[...] = score

    def rank_step(mp, rank):
        r = jnp.broadcast_to(sc_ref[pl.ds(mp, 1), :], (nblk, tq))
        tie = jnp.where(blk > mp, 1.0, 0.0)
        return rank + jnp.where(r > score, 1.0, jnp.where(r == score, tie, 0.0))

    rank = lax.fori_loop(0, jnp.right_shift(q0 + tq - 1, 6) + 1, rank_step, jnp.zeros((nblk, tq), F32))
    sel_t = jnp.where(valid, jnp.where(rank < NSA_TOP_N, 1.0, 0.0), 0.0)
    selp = jnp.concatenate([sel_t, jnp.zeros((LANES - nblk, tq), F32)], axis=0).astype(CDT)

    def flash_init():
        m_ref[...] = jnp.full(m_ref.shape, M_FLOOR, F32)
        acc_ref[...] = jnp.zeros(acc_ref.shape, F32)

    def flash_step(kv, bias):
        s = _dot_nt(kv, qs) + bias
        m_old = m_ref[...]
        m_new = jnp.maximum(m_old, jnp.max(s, axis=0, keepdims=True))
        p = jnp.exp(s - m_new)
        acc_ref[...] = jnp.exp(m_old - m_new) * acc_ref[...] + _dot_tn(ones_k(kv), p.astype(CDT))
        m_ref[...] = m_new

    def flash_out():
        acc = acc_ref[...]
        return acc[Dh:] / jnp.maximum(acc[0:1], TINY)

    def sel_tile(j, diag):
        k0 = pl.multiple_of(j * tk, tk)
        kv = kvs_ref[pl.ds(k0, tk), :]
        bias = (_dot(e_ref[j], selp) - 1.0) * (-NEG_INF)
        if diag:
            kpos = k0 + lax.broadcasted_iota(jnp.int32, (tk, 1), 0)
            bias = jnp.where(kpos <= tcol, bias, NEG_INF)
        flash_step(kv, jnp.concatenate([bias] * Hg, axis=1))

    flash_init()
    jd = q0 // tk

    def sel_loop(j, carry):
        sel_tile(j, False)
        return carry

    lax.fori_loop(0, jd, sel_loop, 0)
    sel_tile(jd, True)
    o_s = flash_out()

    k0 = pl.multiple_of(jnp.maximum(q0 - NSA_WINDOW, 0), tq)
    kv = kvw_ref[pl.ds(k0, tkw), :]
    d = tcol_r - (k0 + lax.broadcasted_iota(jnp.int32, (tkw, 1), 0))
    s = jnp.where((d >= 0) & (d < NSA_WINDOW), _dot_nt(kv, qs), NEG_INF)
    p = jnp.exp(s - jnp.maximum(jnp.max(s, axis=0, keepdims=True), M_FLOOR))
    acc = _dot_tn(ones_k(kv), p.astype(CDT))
    o_w = acc[Dh:] / jnp.maximum(acc[0:1], TINY)

    gt = g_ref[0].T
    outs = []
    for h in range(Hg):
        cs = slice(h * tq, (h + 1) * tq)
        outs.append(gt[3 * h:3 * h + 1] * o_c[:, cs] + gt[3 * h + 1:3 * h + 2] * o_s[:, cs]
                    + gt[3 * h + 2:3 * h + 3] * o_w[:, cs])
    o_ref[...] = jnp.concatenate(outs, axis=0).T.astype(o_ref.dtype)


def _nsa_attention(q, kvc, kvs, kvw, gates, B, S):
    tq, tk, tkw = TQ_NSA, TK_NSA, TKW_NSA
    G = NSA_KV_GROUPS
    nq = S // tq
    ncmp = S // NSA_CMP_STRIDE
    nblk = S // NSA_SEL_LEN
    key_blk = (np.arange(S) // NSA_SEL_LEN).reshape(S // tk, tk, 1)
    e = jnp.asarray(key_blk == np.arange(LANES).reshape(1, 1, LANES), CDT)
    n_cmp = (S - NSA_CMP_LEN) // NSA_CMP_STRIDE + 1
    cs = np.arange(ncmp)[None, :] * NSA_CMP_STRIDE
    bs = np.arange(nblk)[:, None] * NSA_SEL_LEN
    ov = np.clip(np.minimum(cs + NSA_CMP_LEN, bs + NSA_SEL_LEN) - np.maximum(cs, bs), 0, None) / NSA_CMP_LEN
    ov[:, n_cmp:] = 0.0
    ovt = jnp.asarray(ov, jnp.bfloat16)
    R = NSA_HG * tq
    return pl.pallas_call(
        functools.partial(_nsa_body, tq=tq, tk=tk, tkw=tkw, ncmp=ncmp),
        grid=(B, G, nq),
        in_specs=[pl.BlockSpec((tq, NSA_HG * NSA_HEAD_DIM), lambda b, g, i: (b * nq + i, g)),
                  pl.BlockSpec((1, ncmp, LANES), lambda b, g, i: (b, 0, g)),
                  pl.BlockSpec((S, LANES), lambda b, g, i: (b, g)),
                  pl.BlockSpec((S, LANES), lambda b, g, i: (b, g)),
                  pl.BlockSpec((1, tq, LANES), lambda b, g, i: (g, b * nq + i, 0)),
                  _full(e.shape), _full(ovt.shape)],
        out_specs=pl.BlockSpec((tq, NSA_HG * NSA_HEAD_DIM), lambda b, g, i: (b * nq + i, g)),
        out_shape=jax.ShapeDtypeStruct((B * S, NSA_HEADS * NSA_HEAD_DIM), CDT),
        scratch_shapes=[pltpu.VMEM((nblk, tq), F32), pltpu.VMEM((1, R), F32), pltpu.VMEM((LANES, R), F32)],
        compiler_params=_cparams(("parallel", "parallel", "arbitrary")),
        name="nsa_attention",
    )(q, kvc, kvs, kvw, gates, e, ovt)


def kernel(x, positions, ffn_norm, ffn_w_gate, ffn_w_up, ffn_w_down, mix_norm, hy_w_in, mla_q_norm, mla_w_uq, mla_kv_norm, mla_w_ukv, gla_w_a2, gla_b_a, gla_out_norm, hy_w_out, nsa_w_in, nsa_pos_k, nsa_pos_v, nsa_ck_w1, nsa_ck_w2, nsa_cv_w1, nsa_cv_w2, nsa_w_out, final_norm):
    B, S, D = x.shape
    T = B * S
    depth = ffn_norm.shape[0]
    h = x.reshape(T, D)
    tabs = _rope_tables(positions)
    for layer in range(depth):
        j = layer // 2
        h = _ffn(h, ffn_norm[layer, 0], ffn_w_gate[layer, 0], ffn_w_up[layer, 0], ffn_w_down[layer, 0])
        if layer % 2 == 0:
            q, k, v, gq, gk, gv, la, gr = _hy_prep(h, mix_norm[layer], hy_w_in[j], mla_q_norm[j], mla_w_uq[j],
                                                   mla_kv_norm[j], mla_w_ukv[j], gla_w_a2[j], gla_b_a[j], tabs)
            o_mla = _mla_attention(q, k, v, B, S)
            o_gla = _gla(gq, gk, gv, la, gr, gla_out_norm[j], B, S)
            n_mla = MLA_HEADS * MLA_V
            h = _out_proj(h, [o_mla, o_gla], [hy_w_out[j][:n_mla], hy_w_out[j][n_mla:]])
        else:
            q, kvc, kvs, kvw, gates = _nsa_prep(h, mix_norm[layer], nsa_w_in[j], tabs)
            nseg = S // NSA_CMP_STRIDE
            tc = tabs[3:6].reshape(3, B, nseg, NSA_CMP_STRIDE, LANES)[:, :, 1:, NSA_CMP_STRIDE - 1]
            tabs_cmp = jnp.pad(tc, ((0, 0), (0, 0), (0, 1), (0, 0)))
            kv_cmp = _compress(kvc, nsa_pos_k[j], nsa_pos_v[j], nsa_ck_w1[j], nsa_ck_w2[j], nsa_cv_w1[j],
                               nsa_cv_w2[j], tabs_cmp, B, S)
            o_nsa = _nsa_attention(q, kv_cmp, kvs, kvw, gates, B, S)
            h = _out_proj(h, [o_nsa], [nsa_w_out[j]])
        last = layer == depth - 1
        h = _ffn(h, ffn_norm[layer, 1], ffn_w_gate[layer, 1], ffn_w_up[layer, 1], ffn_w_down[layer, 1],
                 final_gain=final_norm if last else None)
    return h.reshape(B, S, D)
```

```python
import functools

import numpy as np
import jax
import jax.numpy as jnp
from jax import lax
from jax.experimental import pallas as pl
from jax.experimental.pallas import tpu as pltpu

F32 = jnp.float32
CDT = jnp.bfloat16

D_MODEL = 1024
D_FF = 2816
ROPE_THETA = 10000.0
NORM_EPS = 1e-6
NEG_INF = -1e30
TINY = 1e-30
LOG2E = 1.4426950408889634
M_FLOOR = -1e29

MLA_HEADS, MLA_NOPE, MLA_ROPE, MLA_V = 8, 64, 32, 64
MLA_Q_LORA, MLA_KV_LORA = 256, 128
GLA_HEADS, GLA_DK, GLA_DV = 4, 64, 128
GLA_GATE_RANK, GLA_GATE_TAU, GLA_CHUNK = 16, 16.0, 64
NSA_HEADS, NSA_KV_GROUPS, NSA_HEAD_DIM = 16, 4, 64
NSA_HG = NSA_HEADS // NSA_KV_GROUPS
NSA_CMP_LEN, NSA_CMP_STRIDE, NSA_CMP_HIDDEN = 32, 16, 128
NSA_SEL_LEN, NSA_TOP_N, NSA_WINDOW = 64, 16, 512
NSA_FORCE_SCORE = 1e4

LANES = 128
VMEM_LIMIT = 52 * 1024 * 1024

TM_FFN = 512
TM_PROJ = 512
TQ_MLA = 512
TQ_NSA = 256
TK_NSA = 512
TKW_NSA = NSA_WINDOW + TQ_NSA
FF_CHUNK = 256


def _cparams(sem):
    return pltpu.CompilerParams(dimension_semantics=sem, vmem_limit_bytes=VMEM_LIMIT)


def _full(shape):
    n = len(shape)
    return pl.BlockSpec(shape, lambda *_: (0,) * n)


def _rms(x, g):
    return x * lax.rsqrt(jnp.mean(x * x, axis=-1, keepdims=True) + NORM_EPS) * g


def _dot(a, b):
    return jnp.dot(a, b, preferred_element_type=F32)


def _dot_nt(a, b):
    return lax.dot_general(a, b, (((1,), (1,)), ((), ())), preferred_element_type=F32)


def _dot_tn(a, b):
    return lax.dot_general(a, b, (((0,), (0,)), ((), ())), preferred_element_type=F32)


def _split3(x):
    hi = x.astype(jnp.bfloat16)
    r = x - hi.astype(F32)
    mid = r.astype(jnp.bfloat16)
    lo = (r - mid.astype(F32)).astype(jnp.bfloat16)
    return hi, mid, lo


def _silu(x):
    return x * jax.nn.sigmoid(x)


def _rope128(x, c, s1, s2, sh_a, sh_b):
    outs = []
    for j in range(x.shape[1] // LANES):
        xs = x[:, j * LANES:(j + 1) * LANES]
        outs.append(xs * c + pltpu.roll(xs, sh_a, 1) * s1 + pltpu.roll(xs, sh_b, 1) * s2)
    return outs[0] if len(outs) == 1 else jnp.concatenate(outs, axis=1)


def _rope_tab_body(pos_ref, pat_ref, o_ref):
    pos = pos_ref[...].astype(F32)
    for t in range(2):
        ang = pos * pat_ref[3 * t:3 * t + 1, :]
        c, s = jnp.cos(ang), jnp.sin(ang)
        o_ref[3 * t] = c
        o_ref[3 * t + 1] = -s * pat_ref[3 * t + 1:3 * t + 2, :]
        o_ref[3 * t + 2] = s * pat_ref[3 * t + 2:3 * t + 3, :]


def _rope_tables(positions):
    T = positions.size
    tm = 1024
    lane = np.arange(LANES)
    pat = np.zeros((8, LANES), np.float32)
    j = np.where((lane >= 64) & (lane < 80), lane - 64, np.where((lane >= 80) & (lane < 96), lane - 80, 0))
    in_rope = (lane >= 64) & (lane < 96)
    pat_mla = jnp.where(in_rope, jnp.power(ROPE_THETA, -jnp.asarray(j, F32) * (2.0 / MLA_ROPE)), 0.0)
    pat_nsa = jnp.power(ROPE_THETA, -jnp.asarray(lane % 32, F32) * (2.0 / NSA_HEAD_DIM))
    pat[1] = (lane >= 64) & (lane < 80)
    pat[2] = (lane >= 80) & (lane < 96)
    pat[4] = (lane % 64) < 32
    pat[5] = (lane % 64) >= 32
    pat = jnp.asarray(pat).at[0].set(pat_mla).at[3].set(pat_nsa)
    return pl.pallas_call(
        _rope_tab_body,
        grid=(T // tm,),
        in_specs=[pl.BlockSpec((tm, 1), lambda i: (i, 0)), _full((8, LANES))],
        out_specs=pl.BlockSpec((6, tm, LANES), lambda i: (0, i, 0)),
        out_shape=jax.ShapeDtypeStruct((6, T, LANES), F32),
        compiler_params=_cparams(("parallel",)),
        name="rope_tables",
    )(positions.reshape(T, 1), pat)


def _ffn_body(x_ref, g_ref, wg_ref, wu_ref, wd_ref, fg_ref, o_ref, *, final):
    x = x_ref[...]
    xn = _rms(x, g_ref[...]).astype(CDT)
    acc = jnp.zeros(x.shape, F32)
    for c in range(D_FF // FF_CHUNK):
        sl = slice(c * FF_CHUNK, (c + 1) * FF_CHUNK)
        g = _dot(xn, wg_ref[:, sl])
        u = _dot(xn, wu_ref[:, sl])
        acc = acc + _dot((_silu(g) * u).astype(CDT), wd_ref[sl, :])
    y = x + 0.5 * acc
    if final:
        y = _rms(y, fg_ref[...])
    o_ref[...] = y


def _ffn(h, gain, wg, wu, wd, final_gain=None):
    T, D = h.shape
    tm = TM_FFN
    final = final_gain is not None
    fg = final_gain if final else gain
    wspec = lambda shape: pl.BlockSpec(shape, lambda i: (0, 0), pipeline_mode=pl.Buffered(1))
    return pl.pallas_call(
        functools.partial(_ffn_body, final=final),
        grid=(T // tm,),
        in_specs=[pl.BlockSpec((tm, D), lambda i: (i, 0)), _full((1, D)),
                  wspec((D, D_FF)), wspec((D, D_FF)), wspec((D_FF, D)), _full((1, D))],
        out_specs=pl.BlockSpec((tm, D), lambda i: (i, 0)),
        out_shape=jax.ShapeDtypeStruct((T, D), F32),
        compiler_params=_cparams(("parallel",)),
        name="ffn_final" if final else "ffn",
    )(h, gain.reshape(1, D), wg.astype(CDT), wu.astype(CDT), wd.astype(CDT), fg.reshape(1, D))


HY_NP = 2176


def _hy_prep_body(h_ref, mg_ref, wp_ref, qn_ref, wuq_ref, kvn_ref, wuk_ref, wuv_ref, wa2_ref, ba_ref, tab_ref,
                  q_ref, k_ref, v_ref, gq_ref, gk_ref, gv_ref, la_ref, gr_ref):
    u = _rms(h_ref[...], mg_ref[...]).astype(CDT)
    c, s1, s2 = tab_ref[0], tab_ref[1], tab_ref[2]
    half = MLA_ROPE // 2
    cq = _rms(_dot(u, wp_ref[:, 0:256]), qn_ref[...]).astype(CDT)
    q = _rope128(_dot(cq, wuq_ref[...]), c, s1, s2, LANES - half, half)
    q_ref[...] = (q * ((MLA_NOPE + MLA_ROPE) ** -0.5 * LOG2E)).astype(q_ref.dtype)
    ckv = _rms(_dot(u, wp_ref[:, 256:384]), kvn_ref[...]).astype(CDT)
    krot = _rope128(_dot(u, wp_ref[:, 384:512]), c, s1, s2, LANES - half, half)
    k = _dot(ckv, wuk_ref[...])
    k_ref[...] = (k + jnp.concatenate([krot] * MLA_HEADS, axis=1)).astype(k_ref.dtype)
    v_ref[...] = _dot(ckv, wuv_ref[...]).astype(v_ref.dtype)
    gq_ref[...] = _dot(u, wp_ref[:, 512:768]) * GLA_DK ** -0.5
    gk_ref[...] = _dot(u, wp_ref[:, 768:1024])
    gv_ref[...] = _dot(u, wp_ref[:, 1024:1536])
    z = _dot(_dot(u, wp_ref[:, 1536:1664]).astype(CDT), wa2_ref[...]) + ba_ref[...]
    la_ref[...] = (jnp.minimum(z, 0.0) - jnp.log(1.0 + jnp.exp(-jnp.abs(z)))) * (1.0 / GLA_GATE_TAU)
    gr_ref[...] = _dot(u, wp_ref[:, 1664:2176])


def _hy_prep(h, mix_gain, w_in, q_norm, w_uq, kv_norm, w_ukv, w_a2, b_a, tabs):
    T, D = h.shape
    tm = TM_PROJ
    z = lambda n: jnp.zeros((D, n), F32)
    wp = jnp.concatenate([w_in[:, 0:384], z(64), w_in[:, 384:416], z(32), w_in[:, 416:1440],
                          w_in[:, 1440:1456], z(112), w_in[:, 1456:1968]], axis=1).astype(CDT)
    wuq = jnp.pad(w_uq.reshape(MLA_Q_LORA, MLA_HEADS, MLA_NOPE + MLA_ROPE),
                  ((0, 0), (0, 0), (0, LANES - MLA_NOPE - MLA_ROPE))).reshape(MLA_Q_LORA, MLA_HEADS * LANES).astype(CDT)
    wkv = w_ukv.reshape(MLA_KV_LORA, MLA_HEADS, MLA_NOPE + MLA_V)
    wuk = jnp.pad(wkv[..., :MLA_NOPE], ((0, 0), (0, 0), (0, LANES - MLA_NOPE))).reshape(MLA_KV_LORA, MLA_HEADS * LANES).astype(CDT)
    wuv = wkv[..., MLA_NOPE:].reshape(MLA_KV_LORA, MLA_HEADS * MLA_V).astype(CDT)
    wa2 = jnp.pad(w_a2, ((0, LANES - GLA_GATE_RANK), (0, 0))).astype(CDT)
    row = lambda n, dt=F32: pl.BlockSpec((tm, n), lambda i: (i, 0))
    outs = [(MLA_HEADS * LANES, CDT), (MLA_HEADS * LANES, CDT), (MLA_HEADS * MLA_V, CDT),
            (256, F32), (256, F32), (512, F32), (256, F32), (512, F32)]
    return pl.pallas_call(
        _hy_prep_body,
        grid=(T // tm,),
        in_specs=[row(D), _full((1, D)), _full((D, HY_NP)), _full((1, MLA_Q_LORA)), _full(wuq.shape),
                  _full((1, MLA_KV_LORA)), _full(wuk.shape), _full(wuv.shape), _full(wa2.shape), _full((1, 256)),
                  pl.BlockSpec((3, tm, LANES), lambda i: (0, i, 0))],
        out_specs=[row(n) for n, _ in outs],
        out_shape=[jax.ShapeDtypeStruct((T, n), dt) for n, dt in outs],
        compiler_params=_cparams(("parallel",)),
        name="hy_prep",
    )(h, mix_gain.reshape(1, D), wp, q_norm.reshape(1, -1), wuq, kv_norm.reshape(1, -1), wuk, wuv, wa2,
      b_a.reshape(1, -1), tabs)


def _mla_body(q_ref, k_ref, v_ref, o_ref, m_ref, acc_ref, *, tq):
    i = pl.program_id(2)
    krow = lax.broadcasted_iota(jnp.int32, (tq, tq), 0)
    qcol = lax.broadcasted_iota(jnp.int32, (tq, tq), 1)
    causal = krow <= qcol
    vlane = lax.broadcasted_iota(jnp.int32, (tq, 2 * MLA_V), 1)
    outs = []
    for h in range(2):
        q = q_ref[:, h * LANES:(h + 1) * LANES]
        own = (vlane >= h * MLA_V) & (vlane < (h + 1) * MLA_V)
        m_ref[...] = jnp.full(m_ref.shape, M_FLOOR, F32)
        acc_ref[...] = jnp.zeros(acc_ref.shape, F32)

        def tile(j, masked):
            r0 = pl.multiple_of(j * tq, tq)
            k = k_ref[pl.ds(r0, tq), h * LANES:(h + 1) * LANES]
            v1 = jnp.where(own, v_ref[pl.ds(r0, tq), :], jnp.ones((), v_ref.dtype))
            s = _dot_nt(k, q)
            if masked:
                s = jnp.where(causal, s, NEG_INF)
            m_old = m_ref[...]
            m_new = jnp.maximum(m_old, jnp.max(s, axis=0, keepdims=True))
            p = jnp.exp2(s - m_new)
            acc_ref[...] = jnp.exp2(m_old - m_new) * acc_ref[...] + _dot_tn(v1, p.astype(CDT))
            m_ref[...] = m_new

        def body(j, carry):
            tile(j, False)
            return carry

        lax.fori_loop(0, i, body, 0)
        tile(i, True)
        acc = acc_ref[...]
        o_rows = acc[h * MLA_V:(h + 1) * MLA_V]
        l_row = acc[(1 - h) * MLA_V:(1 - h) * MLA_V + 1]
        outs.append(o_rows / jnp.maximum(l_row, TINY))
    o_ref[...] = jnp.concatenate(outs, axis=0).T.astype(o_ref.dtype)


def _mla_attention(q, k, v, B, S):
    tq = TQ_MLA
    nq = S // tq
    return pl.pallas_call(
        functools.partial(_mla_body, tq=tq),
        grid=(B, MLA_HEADS // 2, nq),
        in_specs=[pl.BlockSpec((tq, 2 * LANES), lambda b, hp, i: (b * nq + i, hp)),
                  pl.BlockSpec((S, 2 * LANES), lambda b, hp, i: (b, hp)),
                  pl.BlockSpec((S, 2 * MLA_V), lambda b, hp, i: (b, hp))],
        out_specs=pl.BlockSpec((tq, 2 * MLA_V), lambda b, hp, i: (b * nq + i, hp)),
        out_shape=jax.ShapeDtypeStruct((B * S, MLA_HEADS * MLA_V), CDT),
        scratch_shapes=[pltpu.VMEM((1, tq), F32), pltpu.VMEM((2 * MLA_V, tq), F32)],
        compiler_params=_cparams(("parallel", "parallel", "arbitrary")),
        name="mla_attention",
    )(q, k, v)


def _gla_body(q_ref, k_ref, v_ref, la_ref, r_ref, on_ref, o_ref, st_ref, *, nb):
    C = GLA_CHUNK

    @pl.when(pl.program_id(0) == 0)
    def _():
        st_ref[...] = jnp.zeros(st_ref.shape, F32)

    row = lax.broadcasted_iota(jnp.int32, (C, C), 0)
    col = lax.broadcasted_iota(jnp.int32, (C, C), 1)
    tril = row >= col
    tri = jnp.where(tril, 1.0, 0.0).astype(jnp.bfloat16)
    on = on_ref[...]

    def per_b(b, carry):
        hi, mid, lo = _split3(la_ref[b])
        bc = _dot(tri, hi) + _dot(tri, mid) + _dot(tri, lo)
        bl = bc[C - 1:C, :]
        q, k = q_ref[b], k_ref[b]
        k_dec = (k * jnp.exp(bl - bc)).astype(CDT)
        q_inter = (q * jnp.exp(bc)).astype(CDT)
        q_intra = (q * jnp.exp(bc - bl)).astype(CDT)
        dec = jnp.exp(bl)
        for h in range(GLA_HEADS):
            ks = slice(h * GLA_DK, (h + 1) * GLA_DK)
            vs = slice(h * GLA_DV, (h + 1) * GLA_DV)
            v = v_ref[b, :, vs].astype(CDT)
            st = st_ref[b, h]
            a = jnp.where(tril, _dot_nt(q_intra[:, ks], k_dec[:, ks]), 0.0)
            o = _dot(a.astype(CDT), v) + _dot_nt(q_inter[:, ks], st.astype(CDT))
            st_ref[b, h] = st * dec[:, ks] + _dot_tn(v, k_dec[:, ks])
            o_ref[b, :, vs] = (_rms(o, on) * _silu(r_ref[b, :, vs])).astype(o_ref.dtype)
        return carry

    lax.fori_loop(0, nb, per_b, 0, unroll=True)


def _gla(gq, gk, gv, la, gr, out_norm, B, S):
    C = GLA_CHUNK
    blk = lambda n: pl.BlockSpec((B, C, n), lambda c: (0, c, 0))
    r3 = lambda a: a.reshape(B, S, a.shape[-1])
    return pl.pallas_call(
        functools.partial(_gla_body, nb=B),
        grid=(S // C,),
        in_specs=[blk(256), blk(256), blk(512), blk(256), blk(512), _full((1, GLA_DV))],
        out_specs=blk(512),
        out_shape=jax.ShapeDtypeStruct((B, S, GLA_HEADS * GLA_DV), CDT),
        scratch_shapes=[pltpu.VMEM((B, GLA_HEADS, GLA_DV, GLA_DK), F32)],
        compiler_params=_cparams(("arbitrary",)),
        name="gla",
    )(r3(gq), r3(gk), r3(gv), r3(la), r3(gr), out_norm.reshape(1, GLA_DV)).reshape(B * S, GLA_HEADS * GLA_DV)


def _out_proj_body(*refs, n_in):
    h_ref, xs, ws, o_ref = refs[0], refs[1:1 + n_in], refs[1 + n_in:1 + 2 * n_in], refs[-1]
    y = h_ref[...]
    for x_ref, w_ref in zip(xs, ws):
        y = y + _dot(x_ref[...], w_ref[...])
    o_ref[...] = y


def _out_proj(h, xs, ws):
    T, D = h.shape
    tm = TM_PROJ
    return pl.pallas_call(
        functools.partial(_out_proj_body, n_in=len(xs)),
        grid=(T // tm,),
        in_specs=[pl.BlockSpec((tm, D), lambda i: (i, 0))]
        + [pl.BlockSpec((tm, x.shape[1]), lambda i: (i, 0)) for x in xs] + [_full(w.shape) for w in ws],
        out_specs=pl.BlockSpec((tm, D), lambda i: (i, 0)),
        out_shape=jax.ShapeDtypeStruct((T, D), F32),
        compiler_params=_cparams(("parallel",)),
        name="out_proj",
    )(h, *xs, *[w.astype(CDT) for w in ws])


NSA_NP = 2688


def _nsa_prep_body(h_ref, mg_ref, wp_ref, tab_ref, q_ref, kvc_ref, kvs_ref, kvw_ref, g_ref):
    u = _rms(h_ref[...], mg_ref[...]).astype(CDT)
    c, s1, s2 = tab_ref[0], tab_ref[1], tab_ref[2]
    half = NSA_HEAD_DIM // 2
    q = _rope128(_dot(u, wp_ref[:, 0:1024]), c, s1, s2, LANES - half, half)
    q_ref[...] = (q * (NSA_HEAD_DIM ** -0.5 * LOG2E)).astype(q_ref.dtype)
    kvc_ref[...] = _dot(u, wp_ref[:, 1024:1536]).astype(kvc_ref.dtype)
    lane = lax.broadcasted_iota(jnp.int32, c.shape, 1)
    is_k = lane < NSA_HEAD_DIM
    ck, s1k, s2k = jnp.where(is_k, c, 1.0), jnp.where(is_k, s1, 0.0), jnp.where(is_k, s2, 0.0)
    kvs_ref[...] = _rope128(_dot(u, wp_ref[:, 1536:2048]), ck, s1k, s2k, LANES - half, half).astype(kvs_ref.dtype)
    kvw_ref[...] = _rope128(_dot(u, wp_ref[:, 2048:2560]), ck, s1k, s2k, LANES - half, half).astype(kvw_ref.dtype)
    gates = jax.nn.sigmoid(_dot(u, wp_ref[:, 2560:2688]))
    per_group = 3 * NSA_HG
    g_ref[0] = gates
    for g in range(1, NSA_KV_GROUPS):
        g_ref[g] = pltpu.roll(gates, LANES - per_group * g, 1)


def _nsa_wp(w_in):
    G, Dh = NSA_KV_GROUPS, NSA_HEAD_DIM
    cols = list(range(0, 1024))
    for base in (1024, 1536, 2048):
        for g in range(G):
            cols += list(range(base + g * Dh, base + (g + 1) * Dh))
            cols += list(range(base + 256 + g * Dh, base + 256 + (g + 1) * Dh))
    cols += list(range(2560, 2608))
    wp = w_in[:, np.asarray(cols)]
    return jnp.pad(wp, ((0, 0), (0, NSA_NP - wp.shape[1]))).astype(CDT)


def _nsa_prep(h, mix_gain, w_in, tabs):
    T, D = h.shape
    tm = TM_PROJ
    row = lambda n: pl.BlockSpec((tm, n), lambda i: (i, 0))
    return pl.pallas_call(
        _nsa_prep_body,
        grid=(T // tm,),
        in_specs=[row(D), _full((1, D)), _full((D, NSA_NP)), pl.BlockSpec((3, tm, LANES), lambda i: (1, i, 0))],
        out_specs=[row(1024), row(512), row(512), row(512),
                   pl.BlockSpec((NSA_KV_GROUPS, tm, LANES), lambda i: (0, i, 0))],
        out_shape=[jax.ShapeDtypeStruct((T, 1024), CDT), jax.ShapeDtypeStruct((T, 512), CDT),
                   jax.ShapeDtypeStruct((T, 512), CDT), jax.ShapeDtypeStruct((T, 512), CDT),
                   jax.ShapeDtypeStruct((NSA_KV_GROUPS, T, LANES), F32)],
        compiler_params=_cparams(("parallel",)),
        name="nsa_prep",
    )(h, mix_gain.reshape(1, D), _nsa_wp(w_in), tabs)


def _compress_body(x_ref, wa_ref, wb_ref, pa_ref, pb_ref, w2_ref, tab_ref, o_ref, *, nseg):
    G = NSA_KV_GROUPS
    nl = NSA_CMP_STRIDE
    posb = _dot(pa_ref[...].astype(CDT), wa_ref[...]) + _dot(pb_ref[...].astype(CDT), wb_ref[...])
    c, s1, s2 = tab_ref[0, 0], tab_ref[1, 0], tab_ref[2, 0]
    lane = lax.broadcasted_iota(jnp.int32, c.shape, 1)
    is_k = lane < NSA_HEAD_DIM
    ck, s1k, s2k = jnp.where(is_k, c, 1.0), jnp.where(is_k, s1, 0.0), jnp.where(is_k, s2, 0.0)
    rows = lax.broadcasted_iota(jnp.int32, (nseg, LANES), 0)
    half = NSA_HEAD_DIM // 2
    for g in range(G):
        xg = jnp.concatenate([x_ref[0, :, (l * G + g) * LANES:(l * G + g + 1) * LANES] for l in range(nl)], axis=1)
        ha = _dot(xg, wa_ref[...])
        hb = _dot(xg, wb_ref[...])
        hid = ha + pltpu.roll(hb, nseg - 1, 0) + posb[0:1, :]
        kv = _dot(_silu(hid).astype(CDT), w2_ref[...])
        kv = _rope128(kv, ck, s1k, s2k, LANES - half, half)
        o_ref[0, :, g * LANES:(g + 1) * LANES] = jnp.where(rows < nseg - 1, kv, 0.0).astype(o_ref.dtype)


def _compress(kvc, pos_k, pos_v, ck_w1, ck_w2, cv_w1, cv_w2, tabs_cmp, B, S):
    G, Dh, Hd = NSA_KV_GROUPS, NSA_HEAD_DIM, NSA_CMP_HIDDEN
    nl = NSA_CMP_STRIDE
    nseg = S // nl
    x = kvc.reshape(B, nseg, nl * G * LANES)

    def half_w(lo):
        wk = ck_w1[lo * Dh:(lo + nl) * Dh].reshape(nl, Dh, Hd)
        wv = cv_w1[lo * Dh:(lo + nl) * Dh].reshape(nl, Dh, Hd)
        top = jnp.concatenate([wk, jnp.zeros_like(wk)], axis=2)
        bot = jnp.concatenate([jnp.zeros_like(wv), wv], axis=2)
        return jnp.concatenate([top, bot], axis=1).reshape(nl * 2 * Dh, 2 * Hd).astype(CDT)

    def half_p(lo):
        p = jnp.concatenate([pos_k[lo:lo + nl], pos_v[lo:lo + nl]], axis=1).reshape(1, nl * 2 * Dh)
        return jnp.broadcast_to(p, (8, nl * 2 * Dh))

    zk = jnp.zeros((Hd, Dh), F32)
    w2 = jnp.concatenate([jnp.concatenate([ck_w2, zk], axis=1), jnp.concatenate([zk, cv_w2], axis=1)], axis=0).astype(CDT)
    return pl.pallas_call(
        functools.partial(_compress_body, nseg=nseg),
        grid=(B,),
        in_specs=[pl.BlockSpec((1, nseg, nl * G * LANES), lambda b: (b, 0, 0)),
                  _full((nl * LANES, 2 * Hd)), _full((nl * LANES, 2 * Hd)),
                  _full((8, nl * LANES)), _full((8, nl * LANES)), _full((2 * Hd, LANES)),
                  pl.BlockSpec((3, 1, nseg, LANES), lambda b: (0, b, 0, 0))],
        out_specs=pl.BlockSpec((1, nseg, G * LANES), lambda b: (b, 0, 0)),
        out_shape=jax.ShapeDtypeStruct((B, nseg, G * LANES), CDT),
        compiler_params=_cparams(("parallel",)),
        name="nsa_compress",
    )(x, half_w(0), half_w(nl), half_p(0), half_p(nl), w2, tabs_cmp)


def _nsa_body(q_ref, kvc_ref, kvs_ref, kvw_ref, g_ref, e_ref, ovt_ref, o_ref,
              sc_ref, m_ref, acc_ref, *, tq, tk, tkw, ncmp):
    i = pl.program_id(2)
    q0 = i * tq
    Hg, Dh = NSA_HG, NSA_HEAD_DIM
    R = Hg * tq
    nblk = ovt_ref.shape[0]
    zpad = jnp.zeros((tq, LANES - Dh), q_ref.dtype)
    qs = jnp.concatenate([jnp.concatenate([q_ref[:, h * Dh:(h + 1) * Dh], zpad], axis=1) for h in range(Hg)], axis=0)
    tcol = q0 + lax.broadcasted_iota(jnp.int32, (1, tq), 1)
    tcol_r = q0 + (lax.broadcasted_iota(jnp.int32, (1, R), 1) & (tq - 1))

    def ones_k(kv):
        lane = lax.broadcasted_iota(jnp.int32, kv.shape, 1)
        return jnp.where(lane < Dh, jnp.ones((), kv.dtype), kv)

    kvc = kvc_ref[0]
    s = _dot_nt(kvc, qs)
    cend = lax.broadcasted_iota(jnp.int32, (ncmp, 1), 0) * NSA_CMP_STRIDE + (NSA_CMP_LEN - 1)
    mk = cend <= tcol_r
    s = jnp.where(mk, s, NEG_INF)
    p = jnp.where(mk, jnp.exp2(s - jnp.max(s, axis=0, keepdims=True)), 0.0)
    pn = p / jnp.maximum(jnp.sum(p, axis=0, keepdims=True), TINY)
    o_c = _dot_tn(kvc, pn.astype(CDT))[Dh:]
    psum = pn[:, 0:tq]
    for h in range(1, Hg):
        psum = psum + pn[:, h * tq:(h + 1) * tq]

    ovt = ovt_ref[...]
    hi, mid, lo = _split3(psum)
    imp = _dot(ovt, hi) + _dot(ovt, mid) + _dot(ovt, lo)
    blk = lax.broadcasted_iota(jnp.int32, (nblk, tq), 0)
    cur = jnp.right_shift(q0 + lax.broadcasted_iota(jnp.int32, (nblk, tq), 1), 6)
    valid = blk <= cur
    forced = (blk == 0) | (blk == cur) | (blk == cur - 1)
    score = jnp.where(valid, jnp.where(forced, NSA_FORCE_SCORE, imp), NEG_INF)
    sc_ref[...] = score

    def rank_step(mp, rank):
        r = jnp.broadcast_to(sc_ref[pl.ds(mp, 1), :], (nblk, tq))
        tie = jnp.where(blk > mp, 1.0, 0.0)
        return rank + jnp.where(r > score, 1.0, jnp.where(r == score, tie, 0.0))

    rank = lax.fori_loop(0, jnp.right_shift(q0 + tq - 1, 6) + 1, rank_step, jnp.zeros((nblk, tq), F32))
    sel_t = jnp.where(valid, jnp.where(rank < NSA_TOP_N, 1.0, 0.0), 0.0)
    selp = jnp.concatenate([sel_t, jnp.zeros((LANES - nblk, tq), F32)], axis=0).astype(CDT)

    def flash_init():
        m_ref[...] = jnp.full(m_ref.shape, M_FLOOR, F32)
        acc_ref[...] = jnp.zeros(acc_ref.shape, F32)

    def flash_step(kv, bias):
        s = _dot_nt(kv, qs) + bias
        m_old = m_ref[...]
        m_new = jnp.maximum(m_old, jnp.max(s, axis=0, keepdims=True))
        p = jnp.exp2(s - m_new)
        acc_ref[...] = jnp.exp2(m_old - m_new) * acc_ref[...] + _dot_tn(ones_k(kv), p.astype(CDT))
        m_ref[...] = m_new

    def flash_out():
        acc = acc_ref[...]
        return acc[Dh:] / jnp.maximum(acc[0:1], TINY)

    def sel_tile(j, diag):
        k0 = pl.multiple_of(j * tk, tk)
        kv = kvs_ref[pl.ds(k0, tk), :]
        bias = (_dot(e_ref[j], selp) - 1.0) * (-NEG_INF)
        if diag:
            kpos = k0 + lax.broadcasted_iota(jnp.int32, (tk, 1), 0)
            bias = jnp.where(kpos <= tcol, bias, NEG_INF)
        flash_step(kv, jnp.concatenate([bias] * Hg, axis=1))

    flash_init()
    jd = q0 // tk

    def sel_loop(j, carry):
        sel_tile(j, False)
        return carry

    lax.fori_loop(0, jd, sel_loop, 0)
    sel_tile(jd, True)
    o_s = flash_out()

    k0 = pl.multiple_of(jnp.maximum(q0 - NSA_WINDOW, 0), tq)
    kv = kvw_ref[pl.ds(k0, tkw), :]
    d = tcol - (k0 + lax.broadcasted_iota(jnp.int32, (tkw, 1), 0))
    bias = jnp.where((d >= 0) & (d < NSA_WINDOW), 0.0, NEG_INF)
    s = _dot_nt(kv, qs) + jnp.concatenate([bias] * Hg, axis=1)
    p = jnp.exp2(s - jnp.maximum(jnp.max(s, axis=0, keepdims=True), M_FLOOR))
    acc = _dot_tn(ones_k(kv), p.astype(CDT))
    o_w = acc[Dh:] / jnp.maximum(acc[0:1], TINY)

    gt = g_ref[0].T
    outs = []
    for h in range(Hg):
        cs = slice(h * tq, (h + 1) * tq)
        outs.append(gt[3 * h:3 * h + 1] * o_c[:, cs] + gt[3 * h + 1:3 * h + 2] * o_s[:, cs]
                    + gt[3 * h + 2:3 * h + 3] * o_w[:, cs])
    o_ref[...] = jnp.concatenate(outs, axis=0).T.astype(o_ref.dtype)


def _nsa_attention(q, kvc, kvs, kvw, gates, B, S):
    tq, tk, tkw = TQ_NSA, TK_NSA, TKW_NSA
    G = NSA_KV_GROUPS
    nq = S // tq
    ncmp = S // NSA_CMP_STRIDE
    nblk = S // NSA_SEL_LEN
    key_blk = (np.arange(S) // NSA_SEL_LEN).reshape(S // tk, tk, 1)
    e = jnp.asarray(key_blk == np.arange(LANES).reshape(1, 1, LANES), CDT)
    n_cmp = (S - NSA_CMP_LEN) // NSA_CMP_STRIDE + 1
    cs = np.arange(ncmp)[None, :] * NSA_CMP_STRIDE
    bs = np.arange(nblk)[:, None] * NSA_SEL_LEN
    ov = np.clip(np.minimum(cs + NSA_CMP_LEN, bs + NSA_SEL_LEN) - np.maximum(cs, bs), 0, None) / NSA_CMP_LEN
    ov[:, n_cmp:] = 0.0
    ovt = jnp.asarray(ov, jnp.bfloat16)
    R = NSA_HG * tq
    return pl.pallas_call(
        functools.partial(_nsa_body, tq=tq, tk=tk, tkw=tkw, ncmp=ncmp),
        grid=(B, G, nq),
        in_specs=[pl.BlockSpec((tq, NSA_HG * NSA_HEAD_DIM), lambda b, g, i: (b * nq + i, g)),
                  pl.BlockSpec((1, ncmp, LANES), lambda b, g, i: (b, 0, g)),
                  pl.BlockSpec((S, LANES), lambda b, g, i: (b, g)),
                  pl.BlockSpec((S, LANES), lambda b, g, i: (b, g)),
                  pl.BlockSpec((1, tq, LANES), lambda b, g, i: (g, b * nq + i, 0)),
                  _full(e.shape), _full(ovt.shape)],
        out_specs=pl.BlockSpec((tq, NSA_HG * NSA_HEAD_DIM), lambda b, g, i: (b * nq + i, g)),
        out_shape=jax.ShapeDtypeStruct((B * S, NSA_HEADS * NSA_HEAD_DIM), CDT),
        scratch_shapes=[pltpu.VMEM((nblk, tq), F32), pltpu.VMEM((1, R), F32), pltpu.VMEM((LANES, R), F32)],
        compiler_params=_cparams(("parallel", "parallel", "arbitrary")),
        name="nsa_attention",
    )(q, kvc, kvs, kvw, gates, e, ovt)


def kernel(x, positions, ffn_norm, ffn_w_gate, ffn_w_up, ffn_w_down, mix_norm, hy_w_in, mla_q_norm, mla_w_uq, mla_kv_norm, mla_w_ukv, gla_w_a2, gla_b_a, gla_out_norm, hy_w_out, nsa_w_in, nsa_pos_k, nsa_pos_v, nsa_ck_w1, nsa_ck_w2, nsa_cv_w1, nsa_cv_w2, nsa_w_out, final_norm):
    B, S, D = x.shape
    T = B * S
    depth = ffn_norm.shape[0]
    h = x.reshape(T, D)
    tabs = _rope_tables(positions)
    for layer in range(depth):
        j = layer // 2
        h = _ffn(h, ffn_norm[layer, 0], ffn_w_gate[layer, 0], ffn_w_up[layer, 0], ffn_w_down[layer, 0])
        if layer % 2 == 0:
            q, k, v, gq, gk, gv, la, gr = _hy_prep(h, mix_norm[layer], hy_w_in[j], mla_q_norm[j], mla_w_uq[j],
                                                   mla_kv_norm[j], mla_w_ukv[j], gla_w_a2[j], gla_b_a[j], tabs)
            o_mla = _mla_attention(q, k, v, B, S)
            o_gla = _gla(gq, gk, gv, la, gr, gla_out_norm[j], B, S)
            n_mla = MLA_HEADS * MLA_V
            h = _out_proj(h, [o_mla, o_gla], [hy_w_out[j][:n_mla], hy_w_out[j][n_mla:]])
        else:
            q, kvc, kvs, kvw, gates = _nsa_prep(h, mix_norm[layer], nsa_w_in[j], tabs)
            nseg = S // NSA_CMP_STRIDE
            tc = tabs[3:6].reshape(3, B, nseg, NSA_CMP_STRIDE, LANES)[:, :, 1:, NSA_CMP_STRIDE - 1]
            tabs_cmp = jnp.pad(tc, ((0, 0), (0, 0), (0, 1), (0, 0)))
            kv_cmp = _compress(kvc, nsa_pos_k[j], nsa_pos_v[j], nsa_ck_w1[j], nsa_ck_w2[j], nsa_cv_w1[j],
                               nsa_cv_w2[j], tabs_cmp, B, S)
            o_nsa = _nsa_attention(q, kv_cmp, kvs, kvw, gates, B, S)
            h = _out_proj(h, [o_nsa], [nsa_w_out[j]])
        last = layer == depth - 1
        h = _ffn(h, ffn_norm[layer, 1], ffn_w_gate[layer, 1], ffn_w_up[layer, 1], ffn_w_down[layer, 1],
                 final_gain=final_norm if last else None)
    return h.reshape(B, S, D)
```

```python
import functools

import numpy as np
import jax
import jax.numpy as jnp
from jax import lax
from jax.experimental import pallas as pl
from jax.experimental.pallas import tpu as pltpu

F32 = jnp.float32
CDT = jnp.bfloat16

D_MODEL = 1024
D_FF = 2816
ROPE_THETA = 10000.0
NORM_EPS = 1e-6
NEG_INF = -1e30
TINY = 1e-30
LOG2E = 1.4426950408889634
M_FLOOR = -1e29

MLA_HEADS, MLA_NOPE, MLA_ROPE, MLA_V = 8, 64, 32, 64
MLA_Q_LORA, MLA_KV_LORA = 256, 128
GLA_HEADS, GLA_DK, GLA_DV = 4, 64, 128
GLA_GATE_RANK, GLA_GATE_TAU, GLA_CHUNK = 16, 16.0, 64
NSA_HEADS, NSA_KV_GROUPS, NSA_HEAD_DIM = 16, 4, 64
NSA_HG = NSA_HEADS // NSA_KV_GROUPS
NSA_CMP_LEN, NSA_CMP_STRIDE, NSA_CMP_HIDDEN = 32, 16, 128
NSA_SEL_LEN, NSA_TOP_N, NSA_WINDOW = 64, 16, 512
NSA_FORCE_SCORE = 1e4

LANES = 128
VMEM_LIMIT = 52 * 1024 * 1024

TM_FFN = 512
TM_PROJ = 512
TQ_MLA = 1024
TQ_NSA = 256
TK_NSA = 512
TKW_NSA = NSA_WINDOW + TQ_NSA
FF_CHUNK = 256


def _cparams(sem):
    return pltpu.CompilerParams(dimension_semantics=sem, vmem_limit_bytes=VMEM_LIMIT)


def _full(shape):
    n = len(shape)
    return pl.BlockSpec(shape, lambda *_: (0,) * n)


def _rms(x, g):
    return x * lax.rsqrt(jnp.mean(x * x, axis=-1, keepdims=True) + NORM_EPS) * g


def _dot(a, b):
    return jnp.dot(a, b, preferred_element_type=F32)


def _dot_nt(a, b):
    return lax.dot_general(a, b, (((1,), (1,)), ((), ())), preferred_element_type=F32)


def _dot_tn(a, b):
    return lax.dot_general(a, b, (((0,), (0,)), ((), ())), preferred_element_type=F32)


def _split3(x):
    hi = x.astype(jnp.bfloat16)
    r = x - hi.astype(F32)
    mid = r.astype(jnp.bfloat16)
    lo = (r - mid.astype(F32)).astype(jnp.bfloat16)
    return hi, mid, lo


def _silu(x):
    return x * jax.nn.sigmoid(x)


def _rope128(x, c, s1, s2, sh_a, sh_b):
    outs = []
    for j in range(x.shape[1] // LANES):
        xs = x[:, j * LANES:(j + 1) * LANES]
        outs.append(xs * c + pltpu.roll(xs, sh_a, 1) * s1 + pltpu.roll(xs, sh_b, 1) * s2)
    return outs[0] if len(outs) == 1 else jnp.concatenate(outs, axis=1)


def _rope_tab_body(pos_ref, pat_ref, o_ref):
    pos = pos_ref[...].astype(F32)
    for t in range(2):
        ang = pos * pat_ref[3 * t:3 * t + 1, :]
        c, s = jnp.cos(ang), jnp.sin(ang)
        o_ref[3 * t] = c
        o_ref[3 * t + 1] = -s * pat_ref[3 * t + 1:3 * t + 2, :]
        o_ref[3 * t + 2] = s * pat_ref[3 * t + 2:3 * t + 3, :]


def _rope_tables(positions):
    T = positions.size
    tm = 1024
    lane = np.arange(LANES)
    pat = np.zeros((8, LANES), np.float32)
    j = np.where((lane >= 64) & (lane < 80), lane - 64, np.where((lane >= 80) & (lane < 96), lane - 80, 0))
    in_rope = (lane >= 64) & (lane < 96)
    pat_mla = jnp.where(in_rope, jnp.power(ROPE_THETA, -jnp.asarray(j, F32) * (2.0 / MLA_ROPE)), 0.0)
    pat_nsa = jnp.power(ROPE_THETA, -jnp.asarray(lane % 32, F32) * (2.0 / NSA_HEAD_DIM))
    pat[1] = (lane >= 64) & (lane < 80)
    pat[2] = (lane >= 80) & (lane < 96)
    pat[4] = (lane % 64) < 32
    pat[5] = (lane % 64) >= 32
    pat = jnp.asarray(pat).at[0].set(pat_mla).at[3].set(pat_nsa)
    return pl.pallas_call(
        _rope_tab_body,
        grid=(T // tm,),
        in_specs=[pl.BlockSpec((tm, 1), lambda i: (i, 0)), _full((8, LANES))],
        out_specs=pl.BlockSpec((6, tm, LANES), lambda i: (0, i, 0)),
        out_shape=jax.ShapeDtypeStruct((6, T, LANES), F32),
        compiler_params=_cparams(("parallel",)),
        name="rope_tables",
    )(positions.reshape(T, 1), pat)


def _ffn_body(x_ref, g_ref, wg_ref, wu_ref, wd_ref, fg_ref, o_ref, *, final):
    x = x_ref[...]
    xn = _rms(x, g_ref[...]).astype(CDT)
    acc = jnp.zeros(x.shape, F32)
    for c in range(D_FF // FF_CHUNK):
        sl = slice(c * FF_CHUNK, (c + 1) * FF_CHUNK)
        g = _dot(xn, wg_ref[:, sl])
        u = _dot(xn, wu_ref[:, sl])
        acc = acc + _dot((_silu(g) * u).astype(CDT), wd_ref[sl, :])
    y = x + 0.5 * acc
    if final:
        y = _rms(y, fg_ref[...])
    o_ref[...] = y


def _ffn(h, gain, wg, wu, wd, final_gain=None):
    T, D = h.shape
    tm = TM_FFN
    final = final_gain is not None
    fg = final_gain if final else gain
    wspec = lambda shape: pl.BlockSpec(shape, lambda i: (0, 0), pipeline_mode=pl.Buffered(1))
    return pl.pallas_call(
        functools.partial(_ffn_body, final=final),
        grid=(T // tm,),
        in_specs=[pl.BlockSpec((tm, D), lambda i: (i, 0)), _full((1, D)),
                  wspec((D, D_FF)), wspec((D, D_FF)), wspec((D_FF, D)), _full((1, D))],
        out_specs=pl.BlockSpec((tm, D), lambda i: (i, 0)),
        out_shape=jax.ShapeDtypeStruct((T, D), F32),
        compiler_params=_cparams(("parallel",)),
        name="ffn_final" if final else "ffn",
    )(h, gain.reshape(1, D), wg.astype(CDT), wu.astype(CDT), wd.astype(CDT), fg.reshape(1, D))


HY_NP = 2176


def _hy_prep_body(h_ref, mg_ref, wp_ref, qn_ref, wuq_ref, kvn_ref, wuk_ref, wuv_ref, wa2_ref, ba_ref, tab_ref,
                  q_ref, k_ref, v_ref, gq_ref, gk_ref, gv_ref, la_ref, gr_ref):
    u = _rms(h_ref[...], mg_ref[...]).astype(CDT)
    c, s1, s2 = tab_ref[0], tab_ref[1], tab_ref[2]
    half = MLA_ROPE // 2
    cq = _rms(_dot(u, wp_ref[:, 0:256]), qn_ref[...]).astype(CDT)
    q = _rope128(_dot(cq, wuq_ref[...]), c, s1, s2, LANES - half, half)
    q_ref[...] = (q * ((MLA_NOPE + MLA_ROPE) ** -0.5 * LOG2E)).astype(q_ref.dtype)
    ckv = _rms(_dot(u, wp_ref[:, 256:384]), kvn_ref[...]).astype(CDT)
    krot = _rope128(_dot(u, wp_ref[:, 384:512]), c, s1, s2, LANES - half, half)
    k = _dot(ckv, wuk_ref[...])
    k_ref[...] = (k + jnp.concatenate([krot] * MLA_HEADS, axis=1)).astype(k_ref.dtype)
    v_ref[...] = _dot(ckv, wuv_ref[...]).astype(v_ref.dtype)
    gq_ref[...] = _dot(u, wp_ref[:, 512:768]) * GLA_DK ** -0.5
    gk_ref[...] = _dot(u, wp_ref[:, 768:1024])
    gv_ref[...] = _dot(u, wp_ref[:, 1024:1536])
    z = _dot(_dot(u, wp_ref[:, 1536:1664]).astype(CDT), wa2_ref[...]) + ba_ref[...]
    la_ref[...] = (jnp.minimum(z, 0.0) - jnp.log(1.0 + jnp.exp(-jnp.abs(z)))) * (1.0 / GLA_GATE_TAU)
    gr_ref[...] = _dot(u, wp_ref[:, 1664:2176])


def _hy_prep(h, mix_gain, w_in, q_norm, w_uq, kv_norm, w_ukv, w_a2, b_a, tabs):
    T, D = h.shape
    tm = TM_PROJ
    z = lambda n: jnp.zeros((D, n), F32)
    wp = jnp.concatenate([w_in[:, 0:384], z(64), w_in[:, 384:416], z(32), w_in[:, 416:1440],
                          w_in[:, 1440:1456], z(112), w_in[:, 1456:1968]], axis=1).astype(CDT)
    wuq = jnp.pad(w_uq.reshape(MLA_Q_LORA, MLA_HEADS, MLA_NOPE + MLA_ROPE),
                  ((0, 0), (0, 0), (0, LANES - MLA_NOPE - MLA_ROPE))).reshape(MLA_Q_LORA, MLA_HEADS * LANES).astype(CDT)
    wkv = w_ukv.reshape(MLA_KV_LORA, MLA_HEADS, MLA_NOPE + MLA_V)
    wuk = jnp.pad(wkv[..., :MLA_NOPE], ((0, 0), (0, 0), (0, LANES - MLA_NOPE))).reshape(MLA_KV_LORA, MLA_HEADS * LANES).astype(CDT)
    wuv = wkv[..., MLA_NOPE:].reshape(MLA_KV_LORA, MLA_HEADS * MLA_V).astype(CDT)
    wa2 = jnp.pad(w_a2, ((0, LANES - GLA_GATE_RANK), (0, 0))).astype(CDT)
    row = lambda n, dt=F32: pl.BlockSpec((tm, n), lambda i: (i, 0))
    outs = [(MLA_HEADS * LANES, CDT), (MLA_HEADS * LANES, CDT), (MLA_HEADS * MLA_V, CDT),
            (256, F32), (256, F32), (512, F32), (256, F32), (512, F32)]
    return pl.pallas_call(
        _hy_prep_body,
        grid=(T // tm,),
        in_specs=[row(D), _full((1, D)), _full((D, HY_NP)), _full((1, MLA_Q_LORA)), _full(wuq.shape),
                  _full((1, MLA_KV_LORA)), _full(wuk.shape), _full(wuv.shape), _full(wa2.shape), _full((1, 256)),
                  pl.BlockSpec((3, tm, LANES), lambda i: (0, i, 0))],
        out_specs=[row(n) for n, _ in outs],
        out_shape=[jax.ShapeDtypeStruct((T, n), dt) for n, dt in outs],
        compiler_params=_cparams(("parallel",)),
        name="hy_prep",
    )(h, mix_gain.reshape(1, D), wp, q_norm.reshape(1, -1), wuq, kv_norm.reshape(1, -1), wuk, wuv, wa2,
      b_a.reshape(1, -1), tabs)


def _mla_body(q_ref, k_ref, v_ref, o_ref, m_ref, acc_ref, *, tq):
    i = pl.program_id(2)
    krow = lax.broadcasted_iota(jnp.int32, (tq, tq), 0)
    qcol = lax.broadcasted_iota(jnp.int32, (tq, tq), 1)
    causal = krow <= qcol
    vlane = lax.broadcasted_iota(jnp.int32, (tq, 2 * MLA_V), 1)
    outs = []
    for h in range(2):
        q = q_ref[:, h * LANES:(h + 1) * LANES]
        own = (vlane >= h * MLA_V) & (vlane < (h + 1) * MLA_V)
        m_ref[...] = jnp.full(m_ref.shape, M_FLOOR, F32)
        acc_ref[...] = jnp.zeros(acc_ref.shape, F32)

        def tile(j, masked):
            r0 = pl.multiple_of(j * tq, tq)
            k = k_ref[pl.ds(r0, tq), h * LANES:(h + 1) * LANES]
            v1 = jnp.where(own, v_ref[pl.ds(r0, tq), :], jnp.ones((), v_ref.dtype))
            s = _dot_nt(k, q)
            if masked:
                s = jnp.where(causal, s, NEG_INF)
            m_old = m_ref[...]
            m_new = jnp.maximum(m_old, jnp.max(s, axis=0, keepdims=True))
            p = jnp.exp2(s - m_new)
            acc_ref[...] = jnp.exp2(m_old - m_new) * acc_ref[...] + _dot_tn(v1, p.astype(CDT))
            m_ref[...] = m_new

        def body(j, carry):
            tile(j, False)
            return carry

        lax.fori_loop(0, i, body, 0)
        tile(i, True)
        acc = acc_ref[...]
        o_rows = acc[h * MLA_V:(h + 1) * MLA_V]
        l_row = acc[(1 - h) * MLA_V:(1 - h) * MLA_V + 1]
        outs.append(o_rows / jnp.maximum(l_row, TINY))
    o_ref[...] = jnp.concatenate(outs, axis=0).T.astype(o_ref.dtype)


def _mla_attention(q, k, v, B, S):
    tq = TQ_MLA
    nq = S // tq
    return pl.pallas_call(
        functools.partial(_mla_body, tq=tq),
        grid=(B, MLA_HEADS // 2, nq),
        in_specs=[pl.BlockSpec((tq, 2 * LANES), lambda b, hp, i: (b * nq + i, hp)),
                  pl.BlockSpec((S, 2 * LANES), lambda b, hp, i: (b, hp)),
                  pl.BlockSpec((S, 2 * MLA_V), lambda b, hp, i: (b, hp))],
        out_specs=pl.BlockSpec((tq, 2 * MLA_V), lambda b, hp, i: (b * nq + i, hp)),
        out_shape=jax.ShapeDtypeStruct((B * S, MLA_HEADS * MLA_V), CDT),
        scratch_shapes=[pltpu.VMEM((1, tq), F32), pltpu.VMEM((2 * MLA_V, tq), F32)],
        compiler_params=_cparams(("parallel", "parallel", "arbitrary")),
        name="mla_attention",
    )(q, k, v)


def _gla_body(q_ref, k_ref, v_ref, la_ref, r_ref, on_ref, o_ref, st_ref, *, nb):
    C = GLA_CHUNK

    @pl.when(pl.program_id(0) == 0)
    def _():
        st_ref[...] = jnp.zeros(st_ref.shape, F32)

    row = lax.broadcasted_iota(jnp.int32, (C, C), 0)
    col = lax.broadcasted_iota(jnp.int32, (C, C), 1)
    tril = row >= col
    tri = jnp.where(tril, 1.0, 0.0).astype(jnp.bfloat16)
    on = on_ref[...]

    def per_b(b, carry):
        hi, mid, lo = _split3(la_ref[b])
        bc = _dot(tri, hi) + _dot(tri, mid) + _dot(tri, lo)
        bl = bc[C - 1:C, :]
        q, k = q_ref[b], k_ref[b]
        k_dec = (k * jnp.exp(bl - bc)).astype(CDT)
        q_inter = (q * jnp.exp(bc)).astype(CDT)
        q_intra = (q * jnp.exp(bc - bl)).astype(CDT)
        dec = jnp.exp(bl)
        for h in range(GLA_HEADS):
            ks = slice(h * GLA_DK, (h + 1) * GLA_DK)
            vs = slice(h * GLA_DV, (h + 1) * GLA_DV)
            v = v_ref[b, :, vs].astype(CDT)
            st = st_ref[b, h]
            a = jnp.where(tril, _dot_nt(q_intra[:, ks], k_dec[:, ks]), 0.0)
            o = _dot(a.astype(CDT), v) + _dot_nt(q_inter[:, ks], st.astype(CDT))
            st_ref[b, h] = st * dec[:, ks] + _dot_tn(v, k_dec[:, ks])
            o_ref[b, :, vs] = (_rms(o, on) * _silu(r_ref[b, :, vs])).astype(o_ref.dtype)
        return carry

    lax.fori_loop(0, nb, per_b, 0, unroll=True)


def _gla(gq, gk, gv, la, gr, out_norm, B, S):
    C = GLA_CHUNK
    blk = lambda n: pl.BlockSpec((B, C, n), lambda c: (0, c, 0))
    r3 = lambda a: a.reshape(B, S, a.shape[-1])
    return pl.pallas_call(
        functools.partial(_gla_body, nb=B),
        grid=(S // C,),
        in_specs=[blk(256), blk(256), blk(512), blk(256), blk(512), _full((1, GLA_DV))],
        out_specs=blk(512),
        out_shape=jax.ShapeDtypeStruct((B, S, GLA_HEADS * GLA_DV), CDT),
        scratch_shapes=[pltpu.VMEM((B, GLA_HEADS, GLA_DV, GLA_DK), F32)],
        compiler_params=_cparams(("arbitrary",)),
        name="gla",
    )(r3(gq), r3(gk), r3(gv), r3(la), r3(gr), out_norm.reshape(1, GLA_DV)).reshape(B * S, GLA_HEADS * GLA_DV)


def _out_proj_body(*refs, n_in):
    h_ref, xs, ws, o_ref = refs[0], refs[1:1 + n_in], refs[1 + n_in:1 + 2 * n_in], refs[-1]
    y = h_ref[...]
    for x_ref, w_ref in zip(xs, ws):
        y = y + _dot(x_ref[...], w_ref[...])
    o_ref[...] = y


def _out_proj(h, xs, ws):
    T, D = h.shape
    tm = TM_PROJ
    return pl.pallas_call(
        functools.partial(_out_proj_body, n_in=len(xs)),
        grid=(T // tm,),
        in_specs=[pl.BlockSpec((tm, D), lambda i: (i, 0))]
        + [pl.BlockSpec((tm, x.shape[1]), lambda i: (i, 0)) for x in xs] + [_full(w.shape) for w in ws],
        out_specs=pl.BlockSpec((tm, D), lambda i: (i, 0)),
        out_shape=jax.ShapeDtypeStruct((T, D), F32),
        compiler_params=_cparams(("parallel",)),
        name="out_proj",
    )(h, *xs, *[w.astype(CDT) for w in ws])


NSA_NP = 2688


def _nsa_prep_body(h_ref, mg_ref, wp_ref, tab_ref, q_ref, kvc_ref, kvs_ref, kvw_ref, g_ref):
    u = _rms(h_ref[...], mg_ref[...]).astype(CDT)
    c, s1, s2 = tab_ref[0], tab_ref[1], tab_ref[2]
    half = NSA_HEAD_DIM // 2
    q = _rope128(_dot(u, wp_ref[:, 0:1024]), c, s1, s2, LANES - half, half)
    q_ref[...] = (q * (NSA_HEAD_DIM ** -0.5 * LOG2E)).astype(q_ref.dtype)
    kvc_ref[...] = _dot(u, wp_ref[:, 1024:1536]).astype(kvc_ref.dtype)
    lane = lax.broadcasted_iota(jnp.int32, c.shape, 1)
    is_k = lane < NSA_HEAD_DIM
    ck, s1k, s2k = jnp.where(is_k, c, 1.0), jnp.where(is_k, s1, 0.0), jnp.where(is_k, s2, 0.0)
    kvs_ref[...] = _rope128(_dot(u, wp_ref[:, 1536:2048]), ck, s1k, s2k, LANES - half, half).astype(kvs_ref.dtype)
    kvw_ref[...] = _rope128(_dot(u, wp_ref[:, 2048:2560]), ck, s1k, s2k, LANES - half, half).astype(kvw_ref.dtype)
    gates = jax.nn.sigmoid(_dot(u, wp_ref[:, 2560:2688]))
    per_group = 3 * NSA_HG
    g_ref[0] = gates
    for g in range(1, NSA_KV_GROUPS):
        g_ref[g] = pltpu.roll(gates, LANES - per_group * g, 1)


def _nsa_wp(w_in):
    G, Dh = NSA_KV_GROUPS, NSA_HEAD_DIM
    cols = list(range(0, 1024))
    for base in (1024, 1536, 2048):
        for g in range(G):
            cols += list(range(base + g * Dh, base + (g + 1) * Dh))
            cols += list(range(base + 256 + g * Dh, base + 256 + (g + 1) * Dh))
    cols += list(range(2560, 2608))
    wp = w_in[:, np.asarray(cols)]
    return jnp.pad(wp, ((0, 0), (0, NSA_NP - wp.shape[1]))).astype(CDT)


def _nsa_prep(h, mix_gain, w_in, tabs):
    T, D = h.shape
    tm = TM_PROJ
    row = lambda n: pl.BlockSpec((tm, n), lambda i: (i, 0))
    return pl.pallas_call(
        _nsa_prep_body,
        grid=(T // tm,),
        in_specs=[row(D), _full((1, D)), _full((D, NSA_NP)), pl.BlockSpec((3, tm, LANES), lambda i: (1, i, 0))],
        out_specs=[row(1024), row(512), row(512), row(512),
                   pl.BlockSpec((NSA_KV_GROUPS, tm, LANES), lambda i: (0, i, 0))],
        out_shape=[jax.ShapeDtypeStruct((T, 1024), CDT), jax.ShapeDtypeStruct((T, 512), CDT),
                   jax.ShapeDtypeStruct((T, 512), CDT), jax.ShapeDtypeStruct((T, 512), CDT),
                   jax.ShapeDtypeStruct((NSA_KV_GROUPS, T, LANES), F32)],
        compiler_params=_cparams(("parallel",)),
        name="nsa_prep",
    )(h, mix_gain.reshape(1, D), _nsa_wp(w_in), tabs)


def _compress_body(x_ref, wa_ref, wb_ref, pa_ref, pb_ref, w2_ref, tab_ref, o_ref, *, nseg):
    G = NSA_KV_GROUPS
    nl = NSA_CMP_STRIDE
    posb = _dot(pa_ref[...].astype(CDT), wa_ref[...]) + _dot(pb_ref[...].astype(CDT), wb_ref[...])
    c, s1, s2 = tab_ref[0, 0], tab_ref[1, 0], tab_ref[2, 0]
    lane = lax.broadcasted_iota(jnp.int32, c.shape, 1)
    is_k = lane < NSA_HEAD_DIM
    ck, s1k, s2k = jnp.where(is_k, c, 1.0), jnp.where(is_k, s1, 0.0), jnp.where(is_k, s2, 0.0)
    rows = lax.broadcasted_iota(jnp.int32, (nseg, LANES), 0)
    half = NSA_HEAD_DIM // 2
    for g in range(G):
        xg = jnp.concatenate([x_ref[0, :, (l * G + g) * LANES:(l * G + g + 1) * LANES] for l in range(nl)], axis=1)
        ha = _dot(xg, wa_ref[...])
        hb = _dot(xg, wb_ref[...])
        hid = ha + pltpu.roll(hb, nseg - 1, 0) + posb[0:1, :]
        kv = _dot(_silu(hid).astype(CDT), w2_ref[...])
        kv = _rope128(kv, ck, s1k, s2k, LANES - half, half)
        o_ref[0, :, g * LANES:(g + 1) * LANES] = jnp.where(rows < nseg - 1, kv, 0.0).astype(o_ref.dtype)


def _compress(kvc, pos_k, pos_v, ck_w1, ck_w2, cv_w1, cv_w2, tabs_cmp, B, S):
    G, Dh, Hd = NSA_KV_GROUPS, NSA_HEAD_DIM, NSA_CMP_HIDDEN
    nl = NSA_CMP_STRIDE
    nseg = S // nl
    x = kvc.reshape(B, nseg, nl * G * LANES)

    def half_w(lo):
        wk = ck_w1[lo * Dh:(lo + nl) * Dh].reshape(nl, Dh, Hd)
        wv = cv_w1[lo * Dh:(lo + nl) * Dh].reshape(nl, Dh, Hd)
        top = jnp.concatenate([wk, jnp.zeros_like(wk)], axis=2)
        bot = jnp.concatenate([jnp.zeros_like(wv), wv], axis=2)
        return jnp.concatenate([top, bot], axis=1).reshape(nl * 2 * Dh, 2 * Hd).astype(CDT)

    def half_p(lo):
        p = jnp.concatenate([pos_k[lo:lo + nl], pos_v[lo:lo + nl]], axis=1).reshape(1, nl * 2 * Dh)
        return jnp.broadcast_to(p, (8, nl * 2 * Dh))

    zk = jnp.zeros((Hd, Dh), F32)
    w2 = jnp.concatenate([jnp.concatenate([ck_w2, zk], axis=1), jnp.concatenate([zk, cv_w2], axis=1)], axis=0).astype(CDT)
    return pl.pallas_call(
        functools.partial(_compress_body, nseg=nseg),
        grid=(B,),
        in_specs=[pl.BlockSpec((1, nseg, nl * G * LANES), lambda b: (b, 0, 0)),
                  _full((nl * LANES, 2 * Hd)), _full((nl * LANES, 2 * Hd)),
                  _full((8, nl * LANES)), _full((8, nl * LANES)), _full((2 * Hd, LANES)),
                  pl.BlockSpec((3, 1, nseg, LANES), lambda b: (0, b, 0, 0))],
        out_specs=pl.BlockSpec((1, nseg, G * LANES), lambda b: (b, 0, 0)),
        out_shape=jax.ShapeDtypeStruct((B, nseg, G * LANES), CDT),
        compiler_params=_cparams(("parallel",)),
        name="nsa_compress",
    )(x, half_w(0), half_w(nl), half_p(0), half_p(nl), w2, tabs_cmp)


def _nsa_body(q_ref, kvc_ref, kvs_ref, kvw_ref, g_ref, e_ref, ovt_ref, o_ref,
              sc_ref, m_ref, acc_ref, *, tq, tk, tkw, ncmp):
    i = pl.program_id(2)
    q0 = i * tq
    Hg, Dh = NSA_HG, NSA_HEAD_DIM
    R = Hg * tq
    nblk = ovt_ref.shape[0]
    zpad = jnp.zeros((tq, LANES - Dh), q_ref.dtype)
    qs = jnp.concatenate([jnp.concatenate([q_ref[:, h * Dh:(h + 1) * Dh], zpad], axis=1) for h in range(Hg)], axis=0)
    tcol = q0 + lax.broadcasted_iota(jnp.int32, (1, tq), 1)
    tcol_r = q0 + (lax.broadcasted_iota(jnp.int32, (1, R), 1) & (tq - 1))

    def ones_k(kv):
        lane = lax.broadcasted_iota(jnp.int32, kv.shape, 1)
        return jnp.where(lane < Dh, jnp.ones((), kv.dtype), kv)

    kvc = kvc_ref[0]
    s = _dot_nt(kvc, qs)
    cend = lax.broadcasted_iota(jnp.int32, (ncmp, 1), 0) * NSA_CMP_STRIDE + (NSA_CMP_LEN - 1)
    mk = cend <= tcol_r
    s = jnp.where(mk, s, NEG_INF)
    p = jnp.where(mk, jnp.exp2(s - jnp.max(s, axis=0, keepdims=True)), 0.0)
    pn = p / jnp.maximum(jnp.sum(p, axis=0, keepdims=True), TINY)
    o_c = _dot_tn(kvc, pn.astype(CDT))[Dh:]
    psum = pn[:, 0:tq]
    for h in range(1, Hg):
        psum = psum + pn[:, h * tq:(h + 1) * tq]

    ovt = ovt_ref[...]
    hi, mid, lo = _split3(psum)
    imp = _dot(ovt, hi) + _dot(ovt, mid) + _dot(ovt, lo)
    blk = lax.broadcasted_iota(jnp.int32, (nblk, tq), 0)
    cur = jnp.right_shift(q0 + lax.broadcasted_iota(jnp.int32, (nblk, tq), 1), 6)
    valid = blk <= cur
    forced = (blk == 0) | (blk == cur) | (blk == cur - 1)
    score = jnp.where(valid, jnp.where(forced, NSA_FORCE_SCORE, imp), NEG_INF)
    sc_ref[...] = score

    def rank_step(mp, rank):
        r = jnp.broadcast_to(sc_ref[pl.ds(mp, 1), :], (nblk, tq))
        tie = jnp.where(blk > mp, 1.0, 0.0)
        return rank + jnp.where(r > score, 1.0, jnp.where(r == score, tie, 0.0))

    rank = lax.fori_loop(0, jnp.right_shift(q0 + tq - 1, 6) + 1, rank_step, jnp.zeros((nblk, tq), F32))
    sel_t = jnp.where(valid, jnp.where(rank < NSA_TOP_N, 1.0, 0.0), 0.0)
    selp = jnp.concatenate([sel_t, jnp.zeros((LANES - nblk, tq), F32)], axis=0).astype(CDT)

    def flash_init():
        m_ref[...] = jnp.full(m_ref.shape, M_FLOOR, F32)
        acc_ref[...] = jnp.zeros(acc_ref.shape, F32)

    def flash_step(kv, bias):
        s = _dot_nt(kv, qs) + bias
        m_old = m_ref[...]
        m_new = jnp.maximum(m_old, jnp.max(s, axis=0, keepdims=True))
        p = jnp.exp2(s - m_new)
        acc_ref[...] = jnp.exp2(m_old - m_new) * acc_ref[...] + _dot_tn(ones_k(kv), p.astype(CDT))
        m_ref[...] = m_new

    def flash_out():
        acc = acc_ref[...]
        return acc[Dh:] / jnp.maximum(acc[0:1], TINY)

    def sel_tile(j, diag):
        k0 = pl.multiple_of(j * tk, tk)
        kv = kvs_ref[pl.ds(k0, tk), :]
        bias = (_dot(e_ref[j], selp) - 1.0) * (-NEG_INF)
        if diag:
            kpos = k0 + lax.broadcasted_iota(jnp.int32, (tk, 1), 0)
            bias = jnp.where(kpos <= tcol, bias, NEG_INF)
        flash_step(kv, jnp.concatenate([bias] * Hg, axis=1))

    flash_init()
    jd = q0 // tk

    def sel_loop(j, carry):
        sel_tile(j, False)
        return carry

    lax.fori_loop(0, jd, sel_loop, 0)
    sel_tile(jd, True)
    o_s = flash_out()

    k0 = pl.multiple_of(jnp.maximum(q0 - NSA_WINDOW, 0), tq)
    kv = kvw_ref[pl.ds(k0, tkw), :]
    d = tcol - (k0 + lax.broadcasted_iota(jnp.int32, (tkw, 1), 0))
    bias = jnp.where((d >= 0) & (d < NSA_WINDOW), 0.0, NEG_INF)
    s = _dot_nt(kv, qs) + jnp.concatenate([bias] * Hg, axis=1)
    p = jnp.exp2(s - jnp.maximum(jnp.max(s, axis=0, keepdims=True), M_FLOOR))
    acc = _dot_tn(ones_k(kv), p.astype(CDT))
    o_w = acc[Dh:] / jnp.maximum(acc[0:1], TINY)

    gt = g_ref[0].T
    outs = []
    for h in range(Hg):
        cs = slice(h * tq, (h + 1) * tq)
        outs.append(gt[3 * h:3 * h + 1] * o_c[:, cs] + gt[3 * h + 1:3 * h + 2] * o_s[:, cs]
                    + gt[3 * h + 2:3 * h + 3] * o_w[:, cs])
    o_ref[...] = jnp.concatenate(outs, axis=0).T.astype(o_ref.dtype)


def _nsa_attention(q, kvc, kvs, kvw, gates, B, S):
    tq, tk, tkw = TQ_NSA, TK_NSA, TKW_NSA
    G = NSA_KV_GROUPS
    nq = S // tq
    ncmp = S // NSA_CMP_STRIDE
    nblk = S // NSA_SEL_LEN
    key_blk = (np.arange(S) // NSA_SEL_LEN).reshape(S // tk, tk, 1)
    e = jnp.asarray(key_blk == np.arange(LANES).reshape(1, 1, LANES), CDT)
    n_cmp = (S - NSA_CMP_LEN) // NSA_CMP_STRIDE + 1
    cs = np.arange(ncmp)[None, :] * NSA_CMP_STRIDE
    bs = np.arange(nblk)[:, None] * NSA_SEL_LEN
    ov = np.clip(np.minimum(cs + NSA_CMP_LEN, bs + NSA_SEL_LEN) - np.maximum(cs, bs), 0, None) / NSA_CMP_LEN
    ov[:, n_cmp:] = 0.0
    ovt = jnp.asarray(ov, jnp.bfloat16)
    R = NSA_HG * tq
    return pl.pallas_call(
        functools.partial(_nsa_body, tq=tq, tk=tk, tkw=tkw, ncmp=ncmp),
        grid=(B, G, nq),
        in_specs=[pl.BlockSpec((tq, NSA_HG * NSA_HEAD_DIM), lambda b, g, i: (b * nq + i, g)),
                  pl.BlockSpec((1, ncmp, LANES), lambda b, g, i: (b, 0, g)),
                  pl.BlockSpec((S, LANES), lambda b, g, i: (b, g)),
                  pl.BlockSpec((S, LANES), lambda b, g, i: (b, g)),
                  pl.BlockSpec((1, tq, LANES), lambda b, g, i: (g, b * nq + i, 0)),
                  _full(e.shape), _full(ovt.shape)],
        out_specs=pl.BlockSpec((tq, NSA_HG * NSA_HEAD_DIM), lambda b, g, i: (b * nq + i, g)),
        out_shape=jax.ShapeDtypeStruct((B * S, NSA_HEADS * NSA_HEAD_DIM), CDT),
        scratch_shapes=[pltpu.VMEM((nblk, tq), F32), pltpu.VMEM((1, R), F32), pltpu.VMEM((LANES, R), F32)],
        compiler_params=_cparams(("parallel", "parallel", "arbitrary")),
        name="nsa_attention",
    )(q, kvc, kvs, kvw, gates, e, ovt)


def kernel(x, positions, ffn_norm, ffn_w_gate, ffn_w_up, ffn_w_down, mix_norm, hy_w_in, mla_q_norm, mla_w_uq, mla_kv_norm, mla_w_ukv, gla_w_a2, gla_b_a, gla_out_norm, hy_w_out, nsa_w_in, nsa_pos_k, nsa_pos_v, nsa_ck_w1, nsa_ck_w2, nsa_cv_w1, nsa_cv_w2, nsa_w_out, final_norm):
    B, S, D = x.shape
    T = B * S
    depth = ffn_norm.shape[0]
    h = x.reshape(T, D)
    tabs = _rope_tables(positions)
    for layer in range(depth):
        j = layer // 2
        h = _ffn(h, ffn_norm[layer, 0], ffn_w_gate[layer, 0], ffn_w_up[layer, 0], ffn_w_down[layer, 0])
        if layer % 2 == 0:
            q, k, v, gq, gk, gv, la, gr = _hy_prep(h, mix_norm[layer], hy_w_in[j], mla_q_norm[j], mla_w_uq[j],
                                                   mla_kv_norm[j], mla_w_ukv[j], gla_w_a2[j], gla_b_a[j], tabs)
            o_mla = _mla_attention(q, k, v, B, S)
            o_gla = _gla(gq, gk, gv, la, gr, gla_out_norm[j], B, S)
            n_mla = MLA_HEADS * MLA_V
            h = _out_proj(h, [o_mla, o_gla], [hy_w_out[j][:n_mla], hy_w_out[j][n_mla:]])
        else:
            q, kvc, kvs, kvw, gates = _nsa_prep(h, mix_norm[layer], nsa_w_in[j], tabs)
            nseg = S // NSA_CMP_STRIDE
            tc = tabs[3:6].reshape(3, B, nseg, NSA_CMP_STRIDE, LANES)[:, :, 1:, NSA_CMP_STRIDE - 1]
            tabs_cmp = jnp.pad(tc, ((0, 0), (0, 0), (0, 1), (0, 0)))
            kv_cmp = _compress(kvc, nsa_pos_k[j], nsa_pos_v[j], nsa_ck_w1[j], nsa_ck_w2[j], nsa_cv_w1[j],
                               nsa_cv_w2[j], tabs_cmp, B, S)
            o_nsa = _nsa_attention(q, kv_cmp, kvs, kvw, gates, B, S)
            h = _out_proj(h, [o_nsa], [nsa_w_out[j]])
        last = layer == depth - 1
        h = _ffn(h, ffn_norm[layer, 1], ffn_w_gate[layer, 1], ffn_w_up[layer, 1], ffn_w_down[layer, 1],
                 final_gain=final_norm if last else None)
    return h.reshape(B, S, D)
```
